```python
import math
import jax, jax.numpy as jnp
from jax import lax
import numpy as np

D_MODEL = 1024
BATCH = 4
SEQ = 4096
DEPTH = 2
DEC_BATCH = 16
DEC_SEQ = 16
PAST_LEN = 4096

CHUNK = 64
Q_BLOCK = 128
HEAD_DIM = 64
A_HEADS = 8
A_KV_HEADS = 2
A_REP = A_HEADS // A_KV_HEADS
IDX_HEADS = 4
IDX_DIM = 64
TOPK_MAX = 256
B_HEADS = 8
N_BRANCH = 2
ROPE_THETA = 10000.0
N_EXPERTS = 16
N_GROUPS = 4
EXP_PER_GROUP = N_EXPERTS // N_GROUPS
TOP_K = 2
D_EXPERT = 512
RMS_EPS = 1e-6
A_WIDTH = A_HEADS * HEAD_DIM
B_WIDTH = B_HEADS * HEAD_DIM
IN_SIZES = (A_WIDTH, A_KV_HEADS * HEAD_DIM, A_KV_HEADS * HEAD_DIM, IDX_HEADS * IDX_DIM, IDX_DIM,
            IDX_HEADS, B_WIDTH, B_WIDTH, B_WIDTH, N_BRANCH * D_MODEL)
N_IN = sum(IN_SIZES)

kernel_name = "hybrid_dsa_stickbreak_moe_stream_step"


def rms_norm(x, g):
    xf = x.astype(jnp.float32)
    y = xf * lax.rsqrt(jnp.mean(xf * xf, axis=-1, keepdims=True) + RMS_EPS)
    return (y * g.astype(jnp.float32)).astype(x.dtype)


def rope(x, pos):
    d = x.shape[-1]
    inv_freq = ROPE_THETA ** (-jnp.arange(0, d, 2, dtype=jnp.float32) / d)
    ang = pos.astype(jnp.float32)[:, None] * inv_freq[None, :]
    cos = jnp.cos(ang)[:, None, :]
    sin = jnp.sin(ang)[:, None, :]
    xf = x.astype(jnp.float32)
    x1, x2 = xf[..., : d // 2], xf[..., d // 2:]
    return jnp.concatenate([x1 * cos - x2 * sin, x2 * cos + x1 * sin], axis=-1).astype(x.dtype)


def gather_rows(a, idx):
    return jax.vmap(lambda a_b, i_b: a_b[i_b])(a, idx)


def dsa_block(q, qi, wi, pos_q, k, v, ki, pos_k, n_sel):
    b, tq = q.shape[:2]
    dots = jnp.einsum("bqhd,bsd->bqhs", qi, ki).astype(jnp.float32) * (IDX_DIM ** -0.5)
    score = jnp.einsum("bqhs,bqh->bqs", jax.nn.relu(dots), wi.astype(jnp.float32))
    admissible = (pos_k[None, :] // CHUNK) <= (pos_q[:, None] // CHUNK)
    score = jnp.where(admissible[None], score, -jnp.inf)
    top_val, top_idx = lax.top_k(score, n_sel)
    valid = jnp.isfinite(top_val)
    k_sel = gather_rows(k, top_idx)
    v_sel = gather_rows(v, top_idx)
    qg = q.reshape(b, tq, A_KV_HEADS, A_REP, HEAD_DIM)
    logits = jnp.einsum("bqgrd,bqjgd->bqgrj", qg, k_sel).astype(jnp.float32) * (HEAD_DIM ** -0.5)
    logits = jnp.where(valid[:, :, None, None, :], logits, -jnp.inf)
    probs = jax.nn.softmax(logits, axis=-1).astype(v.dtype)
    out = jnp.einsum("bqgrj,bqjgd->bqgrd", probs, v_sel)
    return out.reshape(b, tq, A_HEADS, HEAD_DIM)


def stick_breaking_block(q, pos_q, k, v, pos_k):
    z = jnp.einsum("bqhd,bshd->bhqs", q, k).astype(jnp.float32) * (HEAD_DIM ** -0.5)
    earlier = pos_k[None, :] < pos_q[:, None]
    log_keep = jnp.where(earlier, jax.nn.log_sigmoid(-z), 0.0)
    between = lax.cumsum(log_keep, axis=3, reverse=True) - log_keep
    att = jnp.where(earlier, jnp.exp(jax.nn.log_sigmoid(z) + between), 0.0)
    return jnp.einsum("bhqs,bshd->bqhd", att.astype(v.dtype), v)


def blocked(fn, q_arrays, pos):
    b, t = q_arrays[0].shape[:2]
    nb = t // Q_BLOCK

    def split(a):
        return jnp.moveaxis(a.reshape((b, nb, Q_BLOCK) + a.shape[2:]), 1, 0)

    out = lax.map(lambda args: fn(*args), tuple(split(a) for a in q_arrays) + (pos.reshape(nb, Q_BLOCK),))
    return jnp.moveaxis(out, 0, 1).reshape((b, t) + out.shape[3:])


def token_mixer(h, pos, past, w_in, qn_g, kn_g, w_branch_a, w_branch_b, w_out):
    b, t, _ = h.shape
    split_pts = [int(s) for s in np.cumsum(IN_SIZES)[:-1]]
    qa, ka, va, qi, ki, wi, qb, kb, vb, gl = jnp.split(h @ w_in, split_pts, axis=-1)
    qa = rope(rms_norm(qa.reshape(b, t, A_HEADS, HEAD_DIM), qn_g), pos)
    ka = rope(rms_norm(ka.reshape(b, t, A_KV_HEADS, HEAD_DIM), kn_g), pos)
    va = va.reshape(b, t, A_KV_HEADS, HEAD_DIM)
    qi = rope(qi.reshape(b, t, IDX_HEADS, IDX_DIM), pos)
    ki = rope(ki[:, :, None, :], pos)[:, :, 0, :]
    wi = wi * (IDX_HEADS ** -0.5)
    qb = qb.reshape(b, t, B_HEADS, HEAD_DIM)
    kb = kb.reshape(b, t, B_HEADS, HEAD_DIM)
    vb = vb.reshape(b, t, B_HEADS, HEAD_DIM)
    new_rows = (ka, va, ki, kb, vb)
    if past is None:
        keys = new_rows
        pos_k = pos
    else:
        keys = tuple(jnp.concatenate([p_, n_], axis=1) for p_, n_ in zip(past, new_rows))
        pos_k = jnp.arange(keys[0].shape[1], dtype=jnp.int32)
    k_a, v_a, k_i, k_b, v_b = keys
    n_sel = min(TOPK_MAX, k_a.shape[1] // 4)

    def mix_a(q_, qi_, wi_, pq):
        return dsa_block(q_, qi_, wi_, pq, k_a, v_a, k_i, pos_k, n_sel)

    def mix_b(q_, pq):
        return stick_breaking_block(q_, pq, k_b, v_b, pos_k)

    if past is None:
        o_a = blocked(mix_a, (qa, qi, wi), pos)
        o_b = blocked(mix_b, (qb,), pos)
    else:
        o_a = mix_a(qa, qi, wi, pos)
        o_b = mix_b(qb, pos)
    gates = jax.nn.sigmoid(gl.astype(jnp.float32)).astype(h.dtype).reshape(b, t, N_BRANCH, D_MODEL)
    merged = (gates[:, :, 0] * (o_a.reshape(b, t, A_WIDTH) @ w_branch_a)
              + gates[:, :, 1] * (o_b.reshape(b, t, B_WIDTH) @ w_branch_b))
    return merged @ w_out, new_rows


def grouped_moe(h, w_router, b_router, w_gate, w_up, w_down):
    b, t, d = h.shape
    hf = h.reshape(b * t, d)
    affinity = jax.nn.softmax((hf @ w_router).astype(jnp.float32), axis=-1)
    biased = affinity + b_router.astype(jnp.float32)
    group_score = lax.top_k(biased.reshape(-1, N_GROUPS, EXP_PER_GROUP), TOP_K)[0].sum(-1)
    best_group = jnp.argmax(group_score, axis=-1)
    in_group = (jnp.arange(N_EXPERTS) // EXP_PER_GROUP)[None, :] == best_group[:, None]
    _, top_idx = lax.top_k(jnp.where(in_group, biased, -jnp.inf), TOP_K)
    top_w = jnp.take_along_axis(affinity, top_idx, axis=-1)
    top_w = top_w / jnp.sum(top_w, axis=-1, keepdims=True)
    gates = jnp.sum(jax.nn.one_hot(top_idx, N_EXPERTS, dtype=jnp.float32) * top_w[..., None], axis=1)
    gates = gates.astype(h.dtype)
    y = jnp.zeros_like(hf)
    for e in range(N_EXPERTS):
        hidden = jax.nn.silu(hf @ w_gate[e]) * (hf @ w_up[e])
        y = y + gates[:, e:e + 1] * (hidden @ w_down[e])
    return y.reshape(b, t, d)


def trunk_layer(x, c, pos, past, p):
    mod = jax.nn.silu(c) @ p["w_ada"] + p["b_ada"]
    sh1, sc1, g1, sh2, sc2, g2 = [m[:, None, :] for m in jnp.split(mod, 6, axis=-1)]
    h = rms_norm(x, p["norm1_g"]) * (1 + sc1) + sh1
    mix, rows = token_mixer(h, pos, past, p["w_in"], p["qn_g"], p["kn_g"],
                            p["w_branch_a"], p["w_branch_b"], p["w_out"])
    x = x + g1 * mix
    h = rms_norm(x, p["norm2_g"]) * (1 + sc2) + sh2
    x = x + g2 * grouped_moe(h, p["w_router"], p["b_router"], p["w_exp_gate"], p["w_exp_up"], p["w_exp_down"])
    return x, rows


def setup_inputs(seed: int = 0) -> dict:
    key = jax.random.key(seed)
    ks = jax.random.split(key, 24)

    def nrm(k, shape, scale):
        return jax.random.normal(k, shape, jnp.float32) * scale

    return {
        "x_prompt": nrm(ks[0], (BATCH, SEQ, D_MODEL), 1.0),
        "x_sample": nrm(ks[1], (DEC_BATCH, DEC_SEQ, D_MODEL), 1.0),
        "cache_a_k": nrm(ks[2], (DEPTH, DEC_BATCH, PAST_LEN, A_KV_HEADS, HEAD_DIM), 1.0),
        "cache_a_v": nrm(ks[3], (DEPTH, DEC_BATCH, PAST_LEN, A_KV_HEADS, HEAD_DIM), 1.0),
        "cache_idx_k": nrm(ks[4], (DEPTH, DEC_BATCH, PAST_LEN, IDX_DIM), 1.0),
        "cache_b_k": nrm(ks[5], (DEPTH, DEC_BATCH, PAST_LEN, B_HEADS, HEAD_DIM), 1.0),
        "cache_b_v": nrm(ks[6], (DEPTH, DEC_BATCH, PAST_LEN, B_HEADS, HEAD_DIM), 1.0),
        "c_prompt": nrm(ks[7], (BATCH, D_MODEL), 1.0),
        "c_sample": nrm(ks[8], (DEC_BATCH, D_MODEL), 1.0),
        "norm1_g": 1.0 + nrm(ks[9], (DEPTH, D_MODEL), 0.02),
        "norm2_g": 1.0 + nrm(ks[10], (DEPTH, D_MODEL), 0.02),
        "w_ada": nrm(ks[11], (DEPTH, D_MODEL, 6 * D_MODEL), 0.5 * D_MODEL ** -0.5),
        "b_ada": nrm(ks[12], (DEPTH, 6 * D_MODEL), 0.01),
        "w_in": nrm(ks[13], (DEPTH, D_MODEL, N_IN), D_MODEL ** -0.5),
        "qn_g": 1.0 + nrm(ks[14], (DEPTH, HEAD_DIM), 0.02),
        "kn_g": 1.0 + nrm(ks[15], (DEPTH, HEAD_DIM), 0.02),
        "w_branch_a": nrm(ks[16], (DEPTH, A_WIDTH, D_MODEL), A_WIDTH ** -0.5),
        "w_branch_b": nrm(ks[17], (DEPTH, B_WIDTH, D_MODEL), B_WIDTH ** -0.5),
        "w_out": nrm(ks[18], (DEPTH, D_MODEL, D_MODEL), D_MODEL ** -0.5),
        "w_router": nrm(ks[19], (D_MODEL, N_EXPERTS), D_MODEL ** -0.5),
        "b_router": nrm(ks[20], (N_EXPERTS,), 0.01),
        "w_exp_gate": nrm(ks[21], (DEPTH, N_EXPERTS, D_MODEL, D_EXPERT), D_MODEL ** -0.5),
        "w_exp_up": nrm(ks[22], (DEPTH, N_EXPERTS, D_MODEL, D_EXPERT), D_MODEL ** -0.5),
        "w_exp_down": nrm(ks[23], (DEPTH, N_EXPERTS, D_EXPERT, D_MODEL), D_EXPERT ** -0.5),
    }


def reference(x_prompt, x_sample, cache_a_k, cache_a_v, cache_idx_k, cache_b_k, cache_b_v,
              c_prompt, c_sample, norm1_g, norm2_g, w_ada, b_ada, w_in, qn_g, kn_g,
              w_branch_a, w_branch_b, w_out, w_router, b_router, w_exp_gate, w_exp_up, w_exp_down):
    pos_p = jnp.arange(x_prompt.shape[1], dtype=jnp.int32)
    past_len = cache_a_k.shape[2]
    pos_s = past_len + jnp.arange(x_sample.shape[1], dtype=jnp.int32)
    xp, xs = x_prompt, x_sample
    rows_p, rows_s = [], []
    for l in range(DEPTH):
        p = {
            "norm1_g": norm1_g[l], "norm2_g": norm2_g[l], "w_ada": w_ada[l], "b_ada": b_ada[l],
            "w_in": w_in[l], "qn_g": qn_g[l], "kn_g": kn_g[l],
            "w_branch_a": w_branch_a[l], "w_branch_b": w_branch_b[l], "w_out": w_out[l],
            "w_router": w_router, "b_router": b_router,
            "w_exp_gate": w_exp_gate[l], "w_exp_up": w_exp_up[l], "w_exp_down": w_exp_down[l],
        }
        xp, rp = trunk_layer(xp, c_prompt, pos_p, None, p)
        past = (cache_a_k[l], cache_a_v[l], cache_idx_k[l], cache_b_k[l], cache_b_v[l])
        xs, rs = trunk_layer(xs, c_sample, pos_s, past, p)
        rows_p.append(rp)
        rows_s.append(rs)

    def stack(rows, i):
        return jnp.stack([r[i] for r in rows], axis=0)

    new_a_k_p, new_a_v_p, new_idx_k_p = stack(rows_p, 0), stack(rows_p, 1), stack(rows_p, 2)
    new_b_k_p, new_b_v_p = stack(rows_p, 3), stack(rows_p, 4)
    new_a_k_s, new_a_v_s, new_idx_k_s = stack(rows_s, 0), stack(rows_s, 1), stack(rows_s, 2)
    new_b_k_s, new_b_v_s = stack(rows_s, 3), stack(rows_s, 4)
    return (xp, xs, new_a_k_p, new_a_v_p, new_idx_k_p, new_b_k_p, new_b_v_p,
            new_a_k_s, new_a_v_s, new_idx_k_s, new_b_k_s, new_b_v_s)
```

```python
import functools

import jax
import jax.numpy as jnp
import numpy as np
from jax import lax
from jax.experimental import pallas as pl
from jax.experimental.pallas import tpu as pltpu

F32 = jnp.float32
BF16 = jnp.bfloat16
I32 = jnp.int32

HEAD_DIM = 64
A_HEADS = 8
A_KV_HEADS = 2
A_REP = A_HEADS // A_KV_HEADS
IDX_HEADS = 4
IDX_DIM = 64
B_HEADS = 8
CHUNK = 64
TOPK_MAX = 256
ROPE_THETA = 10000.0
N_EXPERTS = 16
N_GROUPS = 4
EXP_PER_GROUP = N_EXPERTS // N_GROUPS
RMS_EPS = 1e-6
LANES = 128
NEG_BIG = -1e30
INT_MIN = -2147483648
KEY_NEG_INF = -2139095041

C_QA = 0
C_KAD = 512
C_VAD = 768
C_QI = 1024
C_KID = 1280
C_WI = 1408
C_QB = 1536
C_KB = 2048
C_VB = 2560
C_GL = 3072
N_EXT = 5120
WI_PAD = 8

VMEM_LIMIT = 56 * 1024 * 1024


def _dot(a, b):
    return jnp.dot(a, b, preferred_element_type=F32)


def _dot_nt(a, b):
    return lax.dot_general(a, b, (((1,), (1,)), ((), ())), preferred_element_type=F32)


def _split(a):
    hi = a.astype(BF16)
    lo = (a - hi.astype(F32)).astype(BF16)
    return hi, lo


def _dot3(a, b):
    ah, al = _split(a)
    bh, bl = _split(b)
    return _dot(ah, bh) + _dot(al, bh) + _dot(ah, bl)


def _sort_key(s):
    bits = lax.bitcast_convert_type(s, I32)
    return bits ^ ((bits >> 31) & 0x7FFFFFFF)


def _params(*sem):
    return pltpu.CompilerParams(dimension_semantics=sem, vmem_limit_bytes=VMEM_LIMIT)


def _ada_kernel(c_ref, w_ref, b_ref, o_ref):
    c = c_ref[...]
    s = c * jax.nn.sigmoid(c)
    o_ref[...] = _dot3(s, w_ref[...]) + b_ref[...]


def _ada(c_all, w_ada, b_ada):
    depth, d, _ = w_ada.shape
    nc = c_all.shape[0]
    return pl.pallas_call(
        _ada_kernel,
        grid=(depth, 6),
        in_specs=[
            pl.BlockSpec((nc, d), lambda l, j: (0, 0)),
            pl.BlockSpec((None, d, d), lambda l, j: (l, 0, j)),
            pl.BlockSpec((None, 1, d), lambda l, j: (l, 0, j)),
        ],
        out_specs=pl.BlockSpec((None, None, nc, d), lambda l, j: (l, j, 0, 0)),
        out_shape=jax.ShapeDtypeStruct((depth, 6, nc, d), F32),
        compiler_params=_params("arbitrary", "arbitrary"),
        name="ada_mod",
    )(c_all, w_ada, b_ada.reshape(depth, 1, 6 * d))


def _rope(v, cos, sin, first_half):
    partner = jnp.where(first_half, pltpu.roll(v, LANES - 32, 1), pltpu.roll(v, 32, 1))
    return v * cos + partner * sin


def _in_proj_kernel(x_ref, sh_ref, sc_ref, g_ref, w_ref, cos_ref, sin_ref, qn_ref, kn_ref, seg_ref,
                    qa_o, kad_o, vad_o, qi_o, kid_o, wi_o, qb_o, kbh_o, vbh_o, gate_o,
                    nak_o, nav_o, nik_o, nbk_o, nbv_o):
    nb, tr, d = x_ref.shape
    tm = nb * tr
    x = x_ref[...]
    ms = jnp.mean(x * x, axis=-1, keepdims=True)
    h = (x * lax.rsqrt(ms + RMS_EPS) * g_ref[...]) * (1.0 + sc_ref[...]) + sh_ref[...]
    hb = h.reshape(tm, d).astype(BF16)

    def proj(lo, width):
        return _dot(hb, w_ref[:, lo:lo + width])

    cos = cos_ref[...]
    sin = sin_ref[...]
    seg = seg_ref[...]
    lane = lax.broadcasted_iota(I32, (tm, LANES), 1)
    first_half = (lane & 32) == 0
    low64 = lane < 64

    def head_norm(v, gain):
        v2 = v * v
        v2h, v2l = _split(v2)
        msq = (_dot(v2h, seg) + _dot(v2l, seg)) * (1.0 / HEAD_DIM)
        return v * lax.rsqrt(msq + RMS_EPS) * gain

    qn = qn_ref[...]
    kn = kn_ref[...]
    v = proj(C_QA, 512)
    for c in range(4):
        blk = _rope(head_norm(v[:, c * LANES:(c + 1) * LANES], qn), cos, sin, first_half)
        qa_o[:, c * LANES:(c + 1) * LANES] = (blk * (HEAD_DIM ** -0.5)).astype(BF16)
    v = proj(C_KAD, 256)
    kd = [_rope(head_norm(v[:, c * LANES:(c + 1) * LANES], kn), cos, sin, first_half) for c in range(2)]
    kad_o[:, 0:LANES] = kd[0].astype(BF16)
    kad_o[:, LANES:2 * LANES] = kd[1].astype(BF16)
    nak_o[...] = jnp.where(low64, kd[0], kd[1])
    v = proj(C_VAD, 256)
    vad_o[...] = v.astype(BF16)
    nav_o[...] = jnp.where(low64, v[:, 0:LANES], v[:, LANES:2 * LANES])
    v = proj(C_QI, 256)
    for c in range(2):
        blk = _rope(v[:, c * LANES:(c + 1) * LANES], cos, sin, first_half)
        qi_o[:, c * LANES:(c + 1) * LANES] = (blk * (IDX_DIM ** -0.5)).astype(BF16)
    v = _rope(proj(C_KID, LANES), cos, sin, first_half)
    kid_o[...] = v.astype(BF16)
    nik_o[...] = v[:, 0:IDX_DIM]
    v = proj(C_WI, LANES)
    wi_o[...] = v[:, 0:WI_PAD] * (IDX_HEADS ** -0.5)
    qb_o[...] = (proj(C_QB, 512) * (HEAD_DIM ** -0.5)).astype(BF16)
    v = proj(C_KB, 512)
    nbk_o[...] = v
    kbh_o[...] = v.astype(BF16)
    v = proj(C_VB, 512)
    nbv_o[...] = v
    vbh_o[...] = v.astype(BF16)
    for c in range(4):
        gate_o[:, c * 512:(c + 1) * 512] = jax.nn.sigmoid(proj(C_GL + c * 512, 512))


_IN_PROJ_OUTS = (("qa", 512, BF16), ("kad", 256, BF16), ("vad", 256, BF16), ("qi", 256, BF16),
                 ("kid", 128, BF16), ("wi", WI_PAD, F32), ("qb", 512, BF16), ("kbh", 512, BF16),
                 ("vbh", 512, BF16), ("gates", 2048, F32), ("nak", 128, F32), ("nav", 128, F32),
                 ("nik", 64, F32), ("nbk", 512, F32), ("nbv", 512, F32))


def _in_proj(x, mod, layer, boff, norm_g, w_ext, cos, sin, qn, kn, seg, nb, tr):
    b, t, d = x.shape
    tpb = t // tr
    tm = nb * tr
    rows = b * t
    grid = (b // nb) * tpb

    def modspec(which):
        return pl.BlockSpec((None, None, nb, 1, d), lambda i: (layer, which, boff // nb + i // tpb, 0, 0))

    in_specs = [
        pl.BlockSpec((nb, tr, d), lambda i: (i // tpb, i % tpb, 0)),
        modspec(0), modspec(1),
        pl.BlockSpec((None, 1, d), lambda i: (layer, 0, 0)),
        pl.BlockSpec((None, d, N_EXT), lambda i: (layer, 0, 0)),
        pl.BlockSpec((tm, LANES), lambda i: (i % tpb, 0)),
        pl.BlockSpec((tm, LANES), lambda i: (i % tpb, 0)),
        pl.BlockSpec((None, 1, LANES), lambda i: (layer, 0, 0)),
        pl.BlockSpec((None, 1, LANES), lambda i: (layer, 0, 0)),
        pl.BlockSpec((LANES, LANES), lambda i: (0, 0)),
    ]
    out_specs = [pl.BlockSpec((tm, w), lambda i: (i, 0)) for _, w, _ in _IN_PROJ_OUTS]
    out_shape = [jax.ShapeDtypeStruct((rows, w), dt) for _, w, dt in _IN_PROJ_OUTS]
    outs = pl.pallas_call(
        _in_proj_kernel, grid=(grid,), in_specs=in_specs, out_specs=out_specs, out_shape=out_shape,
        compiler_params=_params("arbitrary"), name="in_proj",
    )(x, mod, mod, norm_g, w_ext, cos, sin, qn, kn, seg)
    return {name: o for (name, _, _), o in zip(_IN_PROJ_OUTS, outs)}


def _kth_largest_key(count_ge, rows, n_sel):
    def body(it, v):
        cand = v ^ jnp.left_shift(jnp.int32(1), 31 - it)
        return jnp.where(count_ge(cand) >= n_sel, cand, v)
    return lax.fori_loop(0, 32, body, jnp.full((rows, 1), INT_MIN, I32))


def _lane_fold(m):
    acc = m[:, 0:LANES]
    for c in range(1, m.shape[1] // LANES):
        acc = acc + m[:, c * LANES:(c + 1) * LANES]
    return acc


def _select_top(keys_ref, nkb, rows, tk, n_sel):
    def count_ge(cand):
        def body(kb, acc):
            return acc + _lane_fold(jnp.where(keys_ref[kb] >= cand, 1, 0))
        acc = lax.fori_loop(0, nkb, body, jnp.zeros((rows, LANES), I32))
        return jnp.sum(acc, axis=1, keepdims=True)

    thr = _kth_largest_key(count_ge, rows, n_sel)
    cnt_ge = count_ge(thr)
    need = n_sel - count_ge(thr + 1)
    tied = (thr > KEY_NEG_INF) & (cnt_ge > n_sel)

    @pl.when(jnp.max(jnp.where(tied, 1, 0)) > 0)
    def _():
        r_i = lax.broadcasted_iota(I32, (tk, tk), 0)
        c_i = lax.broadcasted_iota(I32, (tk, tk), 1)
        upper = jnp.where(r_i <= c_i, 1.0, 0.0).astype(BF16)
        need_f = need.astype(F32)

        def body(kb, carry):
            k = keys_ref[kb]
            eq = k == thr
            prefix = _dot(jnp.where(eq, 1.0, 0.0).astype(BF16), upper) + carry
            keys_ref[kb] = jnp.where(eq & (prefix > need_f), thr - 1, k)
            return prefix[:, tk - 1:tk]
        lax.fori_loop(0, nkb, body, jnp.zeros((rows, 1), F32))

    return jnp.maximum(thr, KEY_NEG_INF + 1)


def _mix_a_prompt_kernel(qi_ref, wi_ref, kid_ref, qa_ref, kad_ref, vad_ref, o_ref,
                         keys_ref, m_ref, l_ref, acc_ref, *, tk, n_sel):
    tq = qi_ref.shape[0]
    i = pl.program_id(1)
    nkb = ((i + 1) * tq + tk - 1) // tk
    lane = lax.broadcasted_iota(I32, (1, LANES), 1)
    half_mask = [jnp.where(lane < 64, 1.0, 0.0).astype(BF16), jnp.where(lane >= 64, 1.0, 0.0).astype(BF16)]

    wi = wi_ref[...]
    qim = [qi_ref[:, (h // 2) * LANES:(h // 2 + 1) * LANES] * half_mask[h % 2] for h in range(IDX_HEADS)]
    q_chunk = (i * tq + lax.broadcasted_iota(I32, (tq, 1), 0)) >> 6

    def score_block(kb, _):
        start = pl.multiple_of(kb * tk, tk)
        k = kid_ref[pl.ds(start, tk), :]
        s = jnp.zeros((tq, tk), F32)
        for h in range(IDX_HEADS):
            s = s + jnp.maximum(_dot_nt(qim[h], k), 0.0) * wi[:, h:h + 1]
        k_chunk = (kb * tk + lax.broadcasted_iota(I32, (1, tk), 1)) >> 6
        s = jnp.where(k_chunk <= q_chunk, s, -jnp.inf)
        keys_ref[kb] = _sort_key(s)
        return 0
    lax.fori_loop(0, nkb, score_block, 0)

    thr = _select_top(keys_ref, nkb, tq, tk, n_sel)

    m_ref[...] = jnp.full(m_ref.shape, NEG_BIG, F32)
    l_ref[...] = jnp.zeros(l_ref.shape, F32)
    acc_ref[...] = jnp.zeros(acc_ref.shape, F32)
    qam = [qa_ref[:, (h // 2) * LANES:(h // 2 + 1) * LANES] * half_mask[h % 2] for h in range(A_HEADS)]

    def attn_block(kb, _):
        start = pl.multiple_of(kb * tk, tk)
        sel = keys_ref[kb] >= thr
        for g in range(A_KV_HEADS):
            kblk = kad_ref[pl.ds(start, tk), g * LANES:(g + 1) * LANES]
            vblk = vad_ref[pl.ds(start, tk), g * LANES:(g + 1) * LANES]
            for r in range(A_REP):
                h = g * A_REP + r
                logits = jnp.where(sel, _dot_nt(qam[h], kblk), NEG_BIG)
                m_old = m_ref[h]
                m_new = jnp.maximum(m_old, jnp.max(logits, axis=1, keepdims=True))
                alpha = jnp.exp(m_old - m_new)
                p = jnp.where(sel, jnp.exp(logits - m_new), 0.0)
                l_ref[h] = alpha * l_ref[h] + jnp.sum(p, axis=1, keepdims=True)
                acc_ref[h] = alpha * acc_ref[h] + _dot(p.astype(BF16), vblk)
                m_ref[h] = m_new
        return 0
    lax.fori_loop(0, nkb, attn_block, 0)

    low64 = lax.broadcasted_iota(I32, (tq, LANES), 1) < 64
    for j in range(A_HEADS // 2):
        lo = acc_ref[2 * j] / l_ref[2 * j]
        hi = acc_ref[2 * j + 1] / l_ref[2 * j + 1]
        o_ref[:, j * LANES:(j + 1) * LANES] = jnp.where(low64, lo, hi).astype(BF16)


def _mix_a_prompt(p, b, t, tq, tk):
    nq = t // tq
    n_sel = min(TOPK_MAX, t // 4)
    nkb_max = t // tk
    qspec = lambda w: pl.BlockSpec((tq, w), lambda bb, i: (bb * nq + i, 0))
    kspec = lambda w: pl.BlockSpec((t, w), lambda bb, i: (bb, 0))
    return pl.pallas_call(
        functools.partial(_mix_a_prompt_kernel, tk=tk, n_sel=n_sel),
        grid=(b, nq),
        in_specs=[qspec(256), qspec(WI_PAD), kspec(128), qspec(512), kspec(256), kspec(256)],
        out_specs=qspec(512),
        out_shape=jax.ShapeDtypeStruct((b * t, 512), BF16),
        scratch_shapes=[
            pltpu.VMEM((nkb_max, tq, tk), I32),
            pltpu.VMEM((A_HEADS, tq, 1), F32),
            pltpu.VMEM((A_HEADS, tq, 1), F32),
            pltpu.VMEM((A_HEADS, tq, LANES), F32),
        ],
        compiler_params=_params("arbitrary", "arbitrary"),
        name="mix_a_prompt",
    )(p["qi"], p["wi"], p["kid"], p["qa"], p["kad"], p["vad"])


def _softplus(z):
    return jnp.maximum(z, 0.0) + jnp.log1p(jnp.exp(-jnp.abs(z)))


def _mix_b_prompt_kernel(q_ref, k_ref, v_ref, tri_ref, o_ref, r_ref, acc_ref):
    tq = q_ref.shape[0]
    i = pl.program_id(2)
    lane = lax.broadcasted_iota(I32, (1, LANES), 1)
    half_mask = [jnp.where(lane < 64, 1.0, 0.0).astype(BF16), jnp.where(lane >= 64, 1.0, 0.0).astype(BF16)]
    q = q_ref[...]
    qm = [q * half_mask[0], q * half_mask[1]]
    tri = tri_ref[...]
    r_ref[...] = jnp.zeros(r_ref.shape, F32)
    acc_ref[...] = jnp.zeros(acc_ref.shape, F32)

    def block(kb, diag):
        start = pl.multiple_of(kb * tq, tq)
        kblk = k_ref[pl.ds(start, tq), :]
        vblk = v_ref[pl.ds(start, tq), :]
        if diag:
            earlier = (lax.broadcasted_iota(I32, (tq, tq), 1) < lax.broadcasted_iota(I32, (tq, tq), 0))
        for hh in range(2):
            z = _dot_nt(qm[hh], kblk)
            sp = _softplus(z)
            log_keep = -sp
            if diag:
                log_keep = jnp.where(earlier, log_keep, 0.0)
            lh, ll = _split(log_keep)
            between = _dot(lh, tri) + _dot(ll, tri) + r_ref[hh]
            att = jnp.exp(z - sp + between)
            if diag:
                att = jnp.where(earlier, att, 0.0)
            acc_ref[hh] = acc_ref[hh] + _dot(att.astype(BF16), vblk)
            r_ref[hh] = r_ref[hh] + jnp.sum(log_keep, axis=1, keepdims=True)

    block(i, True)

    def body(n, _):
        block(i - 1 - n, False)
        return 0
    lax.fori_loop(0, i, body, 0)

    low64 = lax.broadcasted_iota(I32, (tq, LANES), 1) < 64
    o_ref[...] = jnp.where(low64, acc_ref[0], acc_ref[1]).astype(BF16)


def _strict_lower(n):
    j = np.arange(n)[:, None]
    s = np.arange(n)[None, :]
    return jnp.asarray((j > s).astype(np.float32), dtype=BF16)


def _mix_b_prompt(p, b, t, tq):
    nq = t // tq
    npair = B_HEADS // 2
    return pl.pallas_call(
        _mix_b_prompt_kernel,
        grid=(b, npair, nq),
        in_specs=[
            pl.BlockSpec((tq, LANES), lambda bb, j, i: (bb * nq + i, j)),
            pl.BlockSpec((t, LANES), lambda bb, j, i: (bb, j)),
            pl.BlockSpec((t, LANES), lambda bb, j, i: (bb, j)),
            pl.BlockSpec((tq, tq), lambda bb, j, i: (0, 0)),
        ],
        out_specs=pl.BlockSpec((tq, LANES), lambda bb, j, i: (bb * nq + i, j)),
        out_shape=jax.ShapeDtypeStruct((b * t, B_HEADS * HEAD_DIM), BF16),
        scratch_shapes=[pltpu.VMEM((2, tq, 1), F32), pltpu.VMEM((2, tq, LANES), F32)],
        compiler_params=_params("arbitrary", "arbitrary", "arbitrary"),
        name="mix_b_prompt",
    )(p["qb"], p["kbh"], p["vbh"], _strict_lower(tq))


def _mix_a_sample_kernel(qi_ref, wi_ref, cki_ref, nki_ref, qa_ref, cka_ref, cva_ref, nka_ref, nva_ref, o_ref,
                         keys_ref, kpad_ref, m_ref, l_ref, acc_ref, *, tk, n_sel):
    t = wi_ref.shape[0]
    past = cki_ref.shape[0]
    nkb = past // tk
    rows = A_REP * t

    wi = wi_ref[...]
    qi = qi_ref[...]
    col = lax.broadcasted_iota(I32, (t, tk), 1)

    def scores(kblk):
        d = _dot_nt(qi, kblk)
        s = jnp.zeros((t, tk), F32)
        for h in range(IDX_HEADS):
            s = s + jnp.maximum(d[h * t:(h + 1) * t, :], 0.0) * wi[:, h:h + 1]
        return s

    def score_block(kb, _):
        start = pl.multiple_of(kb * tk, tk)
        keys_ref[kb] = _sort_key(scores(cki_ref[pl.ds(start, tk), :].astype(BF16)))
        return 0
    lax.fori_loop(0, nkb, score_block, 0)
    kpad_ref[...] = jnp.zeros(kpad_ref.shape, F32)
    kpad_ref[0:t, 0:IDX_DIM] = nki_ref[...]
    s_new = scores(kpad_ref[:, 0:IDX_DIM].astype(BF16))
    q_chunk = (past + lax.broadcasted_iota(I32, (t, 1), 0)) >> 6
    admissible = (col < t) & (((past + col) >> 6) <= q_chunk)
    keys_ref[nkb] = _sort_key(jnp.where(admissible, s_new, -jnp.inf))

    thr = _select_top(keys_ref, nkb + 1, t, tk, n_sel)

    m_ref[...] = jnp.full(m_ref.shape, NEG_BIG, F32)
    l_ref[...] = jnp.zeros(l_ref.shape, F32)
    acc_ref[...] = jnp.zeros(acc_ref.shape, F32)

    thr_rep = jnp.concatenate([thr] * A_REP, axis=0)

    def attend(kb, kblk, vblk):
        sel = jnp.concatenate([keys_ref[kb]] * A_REP, axis=0) >= thr_rep
        for g in range(A_KV_HEADS):
            logits = jnp.where(sel, _dot_nt(qa_ref[g], kblk), NEG_BIG)
            m_old = m_ref[g]
            m_new = jnp.maximum(m_old, jnp.max(logits, axis=1, keepdims=True))
            alpha = jnp.exp(m_old - m_new)
            p = jnp.where(sel, jnp.exp(logits - m_new), 0.0)
            l_ref[g] = alpha * l_ref[g] + jnp.sum(p, axis=1, keepdims=True)
            acc_ref[g] = alpha * acc_ref[g] + _dot(p.astype(BF16), vblk)
            m_ref[g] = m_new

    def attn_block(kb, _):
        start = pl.multiple_of(kb * tk, tk)
        attend(kb, cka_ref[pl.ds(start, tk), :].astype(BF16), cva_ref[pl.ds(start, tk), :].astype(BF16))
        return 0
    lax.fori_loop(0, nkb, attn_block, 0)
    kpad_ref[0:t, :] = nka_ref[...]
    knew = kpad_ref[...].astype(BF16)
    kpad_ref[0:t, :] = nva_ref[...]
    vnew = kpad_ref[...].astype(BF16)
    attend(nkb, knew, vnew)

    for g in range(A_KV_HEADS):
        o_ref[g] = acc_ref[g] / l_ref[g]


def _mix_a_sample(qi_hm, wi, cache_idx_k, nik, qa_g, cache_a_k, cache_a_v, nak, nav, layer, tk):
    b, t, _ = wi.shape
    past = cache_idx_k.shape[2]
    n_sel = min(TOPK_MAX, (past + t) // 4)
    kvw = A_KV_HEADS * HEAD_DIM
    rows = A_REP * t
    per_b = lambda *blk: pl.BlockSpec((None,) + blk, lambda bb: (bb,) + (0,) * len(blk))
    cache = lambda w: pl.BlockSpec((None, None, past, w), lambda bb: (layer, bb, 0, 0))
    return pl.pallas_call(
        functools.partial(_mix_a_sample_kernel, tk=tk, n_sel=n_sel),
        grid=(b,),
        in_specs=[per_b(IDX_HEADS * t, IDX_DIM), per_b(t, WI_PAD), cache(IDX_DIM), per_b(t, IDX_DIM),
                  per_b(A_KV_HEADS, rows, kvw), cache(kvw), cache(kvw), per_b(t, kvw), per_b(t, kvw)],
        out_specs=per_b(A_KV_HEADS, rows, kvw),
        out_shape=jax.ShapeDtypeStruct((b, A_KV_HEADS, rows, kvw), F32),
        scratch_shapes=[
            pltpu.VMEM((past // tk + 1, t, tk), I32),
            pltpu.VMEM((tk, kvw), F32),
            pltpu.VMEM((A_KV_HEADS, rows, 1), F32),
            pltpu.VMEM((A_KV_HEADS, rows, 1), F32),
            pltpu.VMEM((A_KV_HEADS, rows, kvw), F32),
        ],
        compiler_params=_params("arbitrary"),
        name="mix_a_sample",
    )(qi_hm, wi, cache_idx_k, nik, qa_g, cache_a_k.reshape(cache_a_k.shape[:3] + (kvw,)),
      cache_a_v.reshape(cache_a_v.shape[:3] + (kvw,)), nak, nav)


def _mix_b_sample_kernel(wq_ref, ck_ref, cv_ref, nk_ref, nv_ref, tri_ref, o_ref, r_ref, acc_ref, pad_ref, *, t):
    n = pl.program_id(1)
    tk = ck_ref.shape[0]
    wq = wq_ref[...]
    tri = tri_ref[...]

    def block(kblk, vblk, earlier):
        z = _dot(kblk, wq)
        sp = _softplus(z)
        log_keep = -sp
        if earlier is not None:
            log_keep = jnp.where(earlier, log_keep, 0.0)
        lh, ll = _split(log_keep)
        between = _dot(tri, lh) + _dot(tri, ll) + r_ref[...]
        att = jnp.exp(z - sp + between)
        if earlier is not None:
            att = jnp.where(earlier, att, 0.0)
        acc_ref[...] += _dot(att.T.astype(BF16), vblk)
        r_ref[...] += jnp.sum(log_keep, axis=0, keepdims=True)

    @pl.when(n == 0)
    def _():
        r_ref[...] = jnp.zeros(r_ref.shape, F32)
        acc_ref[...] = jnp.zeros(acc_ref.shape, F32)
        key_j = lax.broadcasted_iota(I32, (tk, LANES), 0)
        query_i = lax.broadcasted_iota(I32, (tk, LANES), 1) & (t - 1)
        pad_ref[...] = jnp.zeros(pad_ref.shape, F32)
        pad_ref[0:t, :] = nk_ref[...]
        knew = pad_ref[...].astype(BF16)
        pad_ref[0:t, :] = nv_ref[...]
        vnew = pad_ref[...].astype(BF16)
        block(knew, vnew, key_j < query_i)

    @pl.when(n > 0)
    def _():
        block(ck_ref[...].astype(BF16), cv_ref[...].astype(BF16), None)

    @pl.when(n == pl.num_programs(1) - 1)
    def _():
        width = acc_ref.shape[1]
        head_of_lane = lax.broadcasted_iota(I32, (t, width), 1) >> 6
        out = jnp.zeros((t, width), F32)
        for h in range(B_HEADS):
            out = out + jnp.where(head_of_lane == h, acc_ref[h * t:(h + 1) * t, :], 0.0)
        o_ref[...] = out


def _mix_b_sample(wq, cache_b_k, cache_b_v, nbk, nbv, layer, tk):
    b, t, width = nbk.shape
    assert t & (t - 1) == 0 and B_HEADS * t == LANES
    past = cache_b_k.shape[2]
    nblk = past // tk
    cache = pl.BlockSpec((None, None, tk, width), lambda bb, n: (layer, bb, nblk - jnp.maximum(n, 1), 0))
    per_b = lambda r, w: pl.BlockSpec((None, r, w), lambda bb, n: (bb, 0, 0))
    tri = jnp.asarray((np.arange(tk)[None, :] > np.arange(tk)[:, None]).astype(np.float32), dtype=BF16)
    return pl.pallas_call(
        functools.partial(_mix_b_sample_kernel, t=t),
        grid=(b, nblk + 1),
        in_specs=[per_b(width, LANES), cache, cache, per_b(t, width), per_b(t, width),
                  pl.BlockSpec((tk, tk), lambda bb, n: (0, 0))],
        out_specs=per_b(t, width),
        out_shape=jax.ShapeDtypeStruct((b, t, width), F32),
        scratch_shapes=[pltpu.VMEM((1, LANES), F32), pltpu.VMEM((LANES, width), F32), pltpu.VMEM((tk, width), F32)],
        compiler_params=_params("arbitrary", "arbitrary"),
        name="mix_b_sample",
    )(wq, cache_b_k.reshape(cache_b_k.shape[:3] + (width,)), cache_b_v.reshape(cache_b_v.shape[:3] + (width,)),
      nbk, nbv, tri)


def _merge_kernel(x_ref, oa_ref, ob_ref, gate_ref, g1_ref, sh2_ref, sc2_ref, n2_ref,
                  wa_ref, wb_ref, wo_ref, wr_ref, br_ref, x1_o, h2_o, rg_o):
    nb, tr, d = x_ref.shape
    tm = nb * tr
    a = _dot(oa_ref[...], wa_ref[...])
    bb = _dot(ob_ref[...], wb_ref[...])
    merged = gate_ref[:, 0:d] * a + gate_ref[:, d:2 * d] * bb
    mix = _dot(merged.astype(BF16), wo_ref[...])
    x1 = x_ref[...] + g1_ref[...] * mix.reshape(nb, tr, d)
    x1_o[...] = x1
    ms = jnp.mean(x1 * x1, axis=-1, keepdims=True)
    h2 = ((x1 * lax.rsqrt(ms + RMS_EPS) * n2_ref[...]) * (1.0 + sc2_ref[...]) + sh2_ref[...]).reshape(tm, d)
    h2_o[...] = h2.astype(BF16)

    logits = _dot3(h2, wr_ref[...])
    e = jnp.exp(logits - jnp.max(logits, axis=1, keepdims=True))
    aff = e / jnp.sum(e, axis=1, keepdims=True)
    biased = aff + br_ref[...]
    col = [biased[:, j:j + 1] for j in range(N_EXPERTS)]
    best_val = None
    for g in range(N_GROUPS):
        c = col[g * EXP_PER_GROUP:(g + 1) * EXP_PER_GROUP]
        score = None
        for u in range(EXP_PER_GROUP):
            for w in range(u + 1, EXP_PER_GROUP):
                pair = c[u] + c[w]
                score = pair if score is None else jnp.maximum(score, pair)
        if best_val is None:
            best_val, best_g = score, jnp.zeros((tm, 1), I32)
        else:
            upd = score > best_val
            best_g = jnp.where(upd, g, best_g)
            best_val = jnp.where(upd, score, best_val)
    eidx = lax.broadcasted_iota(I32, (tm, N_EXPERTS), 1)
    cand = jnp.where((eidx >> 2) == best_g, biased, -jnp.inf)
    m1 = jnp.max(cand, axis=1, keepdims=True)
    i1 = jnp.min(jnp.where(cand == m1, eidx, N_EXPERTS), axis=1, keepdims=True)
    cand2 = jnp.where(eidx == i1, -jnp.inf, cand)
    m2 = jnp.max(cand2, axis=1, keepdims=True)
    i2 = jnp.min(jnp.where(cand2 == m2, eidx, N_EXPERTS), axis=1, keepdims=True)
    w1 = jnp.sum(jnp.where(eidx == i1, aff, 0.0), axis=1, keepdims=True)
    w2 = jnp.sum(jnp.where(eidx == i2, aff, 0.0), axis=1, keepdims=True)
    tot = w1 + w2
    rg_o[...] = jnp.where(eidx == i1, w1 / tot, 0.0) + jnp.where(eidx == i2, w2 / tot, 0.0)


def _merge(x, oa, ob, gates, mod, layer, boff, n2, wa, wb, wo, wr, br, nb, tr):
    b, t, d = x.shape
    tpb = t // tr
    tm = nb * tr
    rows = b * t
    grid = (b // nb) * tpb

    def modspec(which):
        return pl.BlockSpec((None, None, nb, 1, d), lambda i: (layer, which, boff // nb + i // tpb, 0, 0))

    xspec = pl.BlockSpec((nb, tr, d), lambda i: (i // tpb, i % tpb, 0))
    row = lambda w: pl.BlockSpec((tm, w), lambda i: (i, 0))
    lw = lambda r, c: pl.BlockSpec((None, r, c), lambda i: (layer, 0, 0))
    return pl.pallas_call(
        _merge_kernel, grid=(grid,),
        in_specs=[xspec, row(512), row(512), row(2 * d), modspec(2), modspec(3), modspec(4), lw(1, d),
                  lw(512, d), lw(512, d), lw(d, d),
                  pl.BlockSpec((d, N_EXPERTS), lambda i: (0, 0)), pl.BlockSpec((1, N_EXPERTS), lambda i: (0, 0))],
        out_specs=[xspec, row(d), row(N_EXPERTS)],
        out_shape=[jax.ShapeDtypeStruct((b, t, d), F32), jax.ShapeDtypeStruct((rows, d), BF16),
                   jax.ShapeDtypeStruct((rows, N_EXPERTS), F32)],
        compiler_params=_params("arbitrary"), name="merge",
    )(x, oa, ob, gates, mod, mod, mod, n2, wa, wb, wo, wr, br)


def _moe_kernel(x_ref, h_ref, rg_ref, g2_ref, wg_ref, wu_ref, wd_ref, o_ref, acc_ref):
    nb, tr, d = x_ref.shape
    tm = nb * tr
    e = pl.program_id(1)

    @pl.when(e == 0)
    def _():
        acc_ref[...] = jnp.zeros(acc_ref.shape, F32)

    h = h_ref[...]
    gate = _dot(h, wg_ref[...])
    up = _dot(h, wu_ref[...])
    hidden = (gate * jax.nn.sigmoid(gate)) * up
    y = _dot(hidden.astype(BF16), wd_ref[...])
    eidx = lax.broadcasted_iota(I32, (tm, N_EXPERTS), 1)
    w = jnp.sum(jnp.where(eidx == e, rg_ref[...], 0.0), axis=1, keepdims=True)
    acc_ref[...] += w * y

    @pl.when(e == N_EXPERTS - 1)
    def _():
        o_ref[...] = x_ref[...] + g2_ref[...] * acc_ref[...].reshape(nb, tr, d)


def _moe(x1, h2, rg, mod, layer, boff, wg, wu, wd, nb, tr):
    b, t, d = x1.shape
    tpb = t // tr
    tm = nb * tr
    grid = (b // nb) * tpb
    de = wg.shape[-1]
    xspec = pl.BlockSpec((nb, tr, d), lambda i, e: (i // tpb, i % tpb, 0))
    return pl.pallas_call(
        _moe_kernel, grid=(grid, N_EXPERTS),
        in_specs=[xspec, pl.BlockSpec((tm, d), lambda i, e: (i, 0)), pl.BlockSpec((tm, N_EXPERTS), lambda i, e: (i, 0)),
                  pl.BlockSpec((None, None, nb, 1, d), lambda i, e: (layer, 5, boff // nb + i // tpb, 0, 0)),
                  pl.BlockSpec((None, None, d, de), lambda i, e: (layer, e, 0, 0)),
                  pl.BlockSpec((None, None, d, de), lambda i, e: (layer, e, 0, 0)),
                  pl.BlockSpec((None, None, de, d), lambda i, e: (layer, e, 0, 0))],
        out_specs=xspec,
        out_shape=jax.ShapeDtypeStruct((b, t, d), F32),
        scratch_shapes=[pltpu.VMEM((tm, d), F32)],
        compiler_params=_params("arbitrary", "arbitrary"), name="moe",
    )(x1, h2, rg, mod, wg, wu, wd)


def _rope_tables(pos):
    inv_freq = ROPE_THETA ** (-jnp.arange(0, HEAD_DIM, 2, dtype=F32) / HEAD_DIM)
    ang = pos.astype(F32)[:, None] * inv_freq[None, :]
    c, s = jnp.cos(ang), jnp.sin(ang)
    cos = jnp.concatenate([c, c, c, c], axis=1)
    sin = jnp.concatenate([-s, s, -s, s], axis=1)
    return cos, sin


def _widen_w_in(w_in):
    sizes = (512, 128, 128, 256, 64, 4, 512, 512, 512, 2048)
    pts = [int(v) for v in np.cumsum(sizes)[:-1]]
    qa, ka, va, qi, ki, wi, qb, kb, vb, gl = jnp.split(w_in, pts, axis=-1)
    dup = lambda w: jnp.concatenate([w[..., 0:64], w[..., 0:64], w[..., 64:128], w[..., 64:128]], axis=-1)
    wi_pad = jnp.pad(wi, ((0, 0), (0, 0), (0, LANES - wi.shape[-1])))
    ext = jnp.concatenate([qa, dup(ka), dup(va), qi, ki, ki, wi_pad, qb, kb, vb, gl], axis=-1)
    assert ext.shape[-1] == N_EXT
    return ext.astype(BF16)


def kernel(x_prompt, x_sample, cache_a_k, cache_a_v, cache_idx_k, cache_b_k, cache_b_v, c_prompt, c_sample,
           norm1_g, norm2_g, w_ada, b_ada, w_in, qn_g, kn_g, w_branch_a, w_branch_b, w_out, w_router, b_router,
           w_exp_gate, w_exp_up, w_exp_down):
    depth = w_in.shape[0]
    bp, tp, d = x_prompt.shape
    bs, ts, _ = x_sample.shape
    past = cache_a_k.shape[2]

    nc = -(-(bs + bp) // 8) * 8
    c_all = jnp.concatenate([c_sample, c_prompt, jnp.zeros((nc - bs - bp, d), F32)], axis=0)
    mod = _ada(c_all, w_ada, b_ada).reshape(depth, 6, nc, 1, d)
    boff_s, boff_p = 0, bs

    w_ext = _widen_w_in(w_in)
    wa, wb, wo = w_branch_a.astype(BF16), w_branch_b.astype(BF16), w_out.astype(BF16)
    wg, wu, wd = w_exp_gate.astype(BF16), w_exp_up.astype(BF16), w_exp_down.astype(BF16)
    n1 = norm1_g.reshape(depth, 1, d)
    n2 = norm2_g.reshape(depth, 1, d)
    qn = jnp.tile(qn_g, (1, 2)).reshape(depth, 1, LANES)
    kn = jnp.tile(kn_g, (1, 2)).reshape(depth, 1, LANES)
    seg = jnp.asarray(np.kron(np.eye(2), np.ones((64, 64))).astype(np.float32), dtype=BF16)
    br = b_router.reshape(1, N_EXPERTS)

    cos_p, sin_p = _rope_tables(jnp.arange(tp, dtype=I32))
    cos_s, sin_s = _rope_tables(past + jnp.arange(ts, dtype=I32))
    cos_s, sin_s = jnp.tile(cos_s, (bs, 1)), jnp.tile(sin_s, (bs, 1))

    tr_p = min(256, tp)
    tq_a = min(256, tp)
    tk_a = min(512, tp)
    tq_b = min(256, tp)
    tm_moe = min(1024, tp)
    tk_s = min(512, past)

    xp, xs = x_prompt, x_sample
    rows_p, rows_s = [], []
    for l in range(depth):
        p = _in_proj(xp, mod, l, boff_p, n1, w_ext, cos_p, sin_p, qn, kn, seg, 1, tr_p)
        oa = _mix_a_prompt(p, bp, tp, tq_a, tk_a)
        ob = _mix_b_prompt(p, bp, tp, tq_b)
        x1, h2, rg = _merge(xp, oa, ob, p["gates"], mod, l, boff_p, n2, wa, wb, wo, w_router, br, 1, tr_p)
        xp = _moe(x1, h2, rg, mod, l, boff_p, wg, wu, wd, 1, tm_moe)
        rows_p.append(p)

        s = _in_proj(xs, mod, l, boff_s, n1, w_ext, cos_s, sin_s, qn, kn, seg, bs, ts)
        qi_hm = s["qi"].reshape(bs, ts, IDX_HEADS, IDX_DIM).transpose(0, 2, 1, 3).reshape(bs, IDX_HEADS * ts, IDX_DIM)
        qa5 = s["qa"].reshape(bs, ts, A_KV_HEADS, A_REP, HEAD_DIM).transpose(0, 2, 3, 1, 4)
        qa5 = qa5.reshape(bs, A_KV_HEADS, A_REP * ts, HEAD_DIM)
        zeros = jnp.zeros_like(qa5[:, 0])
        qa_g = jnp.stack([jnp.concatenate([qa5[:, 0], zeros], axis=-1),
                          jnp.concatenate([zeros, qa5[:, 1]], axis=-1)], axis=1)
        oa_s = _mix_a_sample(qi_hm, s["wi"].reshape(bs, ts, WI_PAD), cache_idx_k, s["nik"].reshape(bs, ts, IDX_DIM),
                             qa_g, cache_a_k, cache_a_v, s["nak"].reshape(bs, ts, 128), s["nav"].reshape(bs, ts, 128),
                             l, tk_s)
        oa_s = jnp.stack([oa_s[:, g, :, g * HEAD_DIM:(g + 1) * HEAD_DIM] for g in range(A_KV_HEADS)], axis=1)
        oa_s = oa_s.reshape(bs, A_KV_HEADS, A_REP, ts, HEAD_DIM).transpose(0, 3, 1, 2, 4).reshape(bs * ts, 512)
        qb4 = s["qb"].reshape(bs, ts, B_HEADS, HEAD_DIM).transpose(0, 2, 3, 1)
        eye = jnp.eye(B_HEADS, dtype=BF16)
        wq = (qb4[:, :, :, None, :] * eye[None, :, None, :, None]).reshape(bs, B_HEADS * HEAD_DIM, B_HEADS * ts)
        ob_s = _mix_b_sample(wq, cache_b_k, cache_b_v, s["nbk"].reshape(bs, ts, 512), s["nbv"].reshape(bs, ts, 512),
                             l, tk_s)
        x1s, h2s, rgs = _merge(xs, oa_s.astype(BF16), ob_s.reshape(bs * ts, 512).astype(BF16), s["gates"], mod, l,
                               boff_s, n2, wa, wb, wo, w_router, br, bs, ts)
        xs = _moe(x1s, h2s, rgs, mod, l, boff_s, wg, wu, wd, bs, ts)
        rows_s.append(s)

    def stack(rows, name, b, t, tail):
        return jnp.stack([r[name].reshape((b, t) + tail) for r in rows], axis=0)

    kv = (A_KV_HEADS, HEAD_DIM)
    bh = (B_HEADS, HEAD_DIM)
    return (xp, xs,
            stack(rows_p, "nak", bp, tp, kv), stack(rows_p, "nav", bp, tp, kv), stack(rows_p, "nik", bp, tp, (IDX_DIM,)),
            stack(rows_p, "nbk", bp, tp, bh), stack(rows_p, "nbv", bp, tp, bh),
            stack(rows_s, "nak", bs, ts, kv), stack(rows_s, "nav", bs, ts, kv), stack(rows_s, "nik", bs, ts, (IDX_DIM,)),
            stack(rows_s, "nbk", bs, ts, bh), stack(rows_s, "nbv", bs, ts, bh))
```

```python
import functools

import jax
import jax.numpy as jnp
import numpy as np
from jax import lax
from jax.experimental import pallas as pl
from jax.experimental.pallas import tpu as pltpu

F32 = jnp.float32
BF16 = jnp.bfloat16
I32 = jnp.int32

HEAD_DIM = 64
A_HEADS = 8
A_KV_HEADS = 2
A_REP = A_HEADS // A_KV_HEADS
IDX_HEADS = 4
IDX_DIM = 64
B_HEADS = 8
CHUNK = 64
TOPK_MAX = 256
ROPE_THETA = 10000.0
N_EXPERTS = 16
N_GROUPS = 4
EXP_PER_GROUP = N_EXPERTS // N_GROUPS
RMS_EPS = 1e-6
LANES = 128
LOG2_E = 1.4426950408889634
A_Q_SCALE = HEAD_DIM ** -0.5 * LOG2_E
NEG_BIG = -1e30
MASKED = -3e30
STICK_CUTOFF = -120.0
INT_MIN = -2147483648
KEY_NEG_INF = -2139095041

C_QA = 0
C_KAD = 512
C_VAD = 768
C_QI = 1024
C_KID = 1280
C_WI = 1408
C_QB = 1536
C_KB = 2048
C_VB = 2560
C_GL = 3072
N_EXT = 5120
WI_PAD = 8

VMEM_LIMIT = 56 * 1024 * 1024


def _dot(a, b):
    return jnp.dot(a, b, preferred_element_type=F32)


def _dot_nt(a, b):
    return lax.dot_general(a, b, (((1,), (1,)), ((), ())), preferred_element_type=F32)


def _split(a):
    hi = a.astype(BF16)
    lo = (a - hi.astype(F32)).astype(BF16)
    return hi, lo


def _dot3(a, b):
    ah, al = _split(a)
    bh, bl = _split(b)
    return _dot(ah, bh) + _dot(al, bh) + _dot(ah, bl)


def _sort_key(s):
    bits = lax.bitcast_convert_type(s, I32)
    return bits ^ ((bits >> 31) & 0x7FFFFFFF)


def _params(*sem):
    return pltpu.CompilerParams(dimension_semantics=sem, vmem_limit_bytes=VMEM_LIMIT)


def _ada_kernel(c_ref, w_ref, b_ref, o_ref):
    c = c_ref[...]
    s = c * jax.nn.sigmoid(c)
    o_ref[...] = _dot3(s, w_ref[...]) + b_ref[...]


def _ada(c_all, w_ada, b_ada):
    depth, d, _ = w_ada.shape
    nc = c_all.shape[0]
    return pl.pallas_call(
        _ada_kernel,
        grid=(depth, 6),
        in_specs=[
            pl.BlockSpec((nc, d), lambda l, j: (0, 0)),
            pl.BlockSpec((None, d, d), lambda l, j: (l, 0, j)),
            pl.BlockSpec((None, 1, d), lambda l, j: (l, 0, j)),
        ],
        out_specs=pl.BlockSpec((None, None, nc, d), lambda l, j: (l, j, 0, 0)),
        out_shape=jax.ShapeDtypeStruct((depth, 6, nc, d), F32),
        compiler_params=_params("arbitrary", "arbitrary"),
        name="ada_mod",
    )(c_all, w_ada, b_ada.reshape(depth, 1, 6 * d))


def _rope(v, cos, sin, first_half):
    partner = jnp.where(first_half, pltpu.roll(v, LANES - 32, 1), pltpu.roll(v, 32, 1))
    return v * cos + partner * sin


_IN_PROJ_ROWS = (("qa", 512, BF16), ("qi", 256, BF16), ("wi", WI_PAD, F32), ("qb", 512, BF16),
                 ("gates", 2048, F32), ("nak", 128, F32), ("nav", 128, F32), ("nik", 64, F32),
                 ("nbk", 512, F32), ("nbv", 512, F32))
_IN_PROJ_PROMPT_ROWS = (("vad", 256, BF16), ("vbh", 512, BF16))
_IN_PROJ_PROMPT_KEYS = (("kadT", 256), ("kidT", 128), ("kbT", 512))
KEY_BLOCK = 256


def _in_proj_kernel(x_ref, sh_ref, sc_ref, g_ref, w_ref, cos_ref, sin_ref, qn_ref, kn_ref, seg_ref, *out_refs,
                    names):
    o = dict(zip(names, out_refs))
    prompt = "kadT" in o
    nb, tr, d = x_ref.shape
    tm = nb * tr
    x = x_ref[...]
    ms = jnp.mean(x * x, axis=-1, keepdims=True)
    h = (x * lax.rsqrt(ms + RMS_EPS) * g_ref[...]) * (1.0 + sc_ref[...]) + sh_ref[...]
    hb = h.reshape(tm, d).astype(BF16)

    def proj(lo, width):
        return _dot(hb, w_ref[:, lo:lo + width])

    cos = cos_ref[...]
    sin = sin_ref[...]
    seg = seg_ref[...]
    lane = lax.broadcasted_iota(I32, (tm, LANES), 1)
    first_half = (lane & 32) == 0
    low64 = lane < 64

    def head_norm(v, gain):
        v2 = v * v
        v2h, v2l = _split(v2)
        msq = (_dot(v2h, seg) + _dot(v2l, seg)) * (1.0 / HEAD_DIM)
        return v * lax.rsqrt(msq + RMS_EPS) * gain

    qn = qn_ref[...]
    kn = kn_ref[...]
    v = proj(C_QA, 512)
    for c in range(4):
        blk = _rope(head_norm(v[:, c * LANES:(c + 1) * LANES], qn), cos, sin, first_half)
        o["qa"][:, c * LANES:(c + 1) * LANES] = (blk * A_Q_SCALE).astype(BF16)
    v = proj(C_KAD, 256)
    kd = [_rope(head_norm(v[:, c * LANES:(c + 1) * LANES], kn), cos, sin, first_half) for c in range(2)]
    o["nak"][...] = jnp.where(low64, kd[0], kd[1])
    v = proj(C_VAD, 256)
    o["nav"][...] = jnp.where(low64, v[:, 0:LANES], v[:, LANES:2 * LANES])
    if prompt:
        for c in range(2):
            o["kadT"][c * LANES:(c + 1) * LANES, :] = kd[c].T.astype(BF16)
        ones_col = jnp.where(lane == 64, 1.0, 0.0)
        for c in range(2):
            blk = jnp.where(low64, v[:, c * LANES:(c + 1) * LANES], ones_col)
            o["vad"][:, c * LANES:(c + 1) * LANES] = blk.astype(BF16)
    v = proj(C_QI, 256)
    for c in range(2):
        blk = _rope(v[:, c * LANES:(c + 1) * LANES], cos, sin, first_half)
        o["qi"][:, c * LANES:(c + 1) * LANES] = (blk * (IDX_DIM ** -0.5)).astype(BF16)
    v = _rope(proj(C_KID, LANES), cos, sin, first_half)
    o["nik"][...] = v[:, 0:IDX_DIM]
    if prompt:
        o["kidT"][...] = v.T.astype(BF16)
    v = proj(C_WI, LANES)
    o["wi"][...] = v[:, 0:WI_PAD] * (IDX_HEADS ** -0.5)
    o["qb"][...] = (proj(C_QB, 512) * (HEAD_DIM ** -0.5)).astype(BF16)
    v = proj(C_KB, 512)
    o["nbk"][...] = v
    if prompt:
        o["kbT"][...] = v.T.astype(BF16)
    v = proj(C_VB, 512)
    o["nbv"][...] = v
    if prompt:
        o["vbh"][...] = v.astype(BF16)
    for c in range(4):
        o["gates"][:, c * 512:(c + 1) * 512] = jax.nn.sigmoid(proj(C_GL + c * 512, 512))


def _in_proj(x, mod, layer, boff, norm_g, w_ext, cos, sin, qn, kn, seg, nb, tr, prompt):
    b, t, d = x.shape
    tpb = t // tr
    tm = nb * tr
    rows = b * t
    grid = (b // nb) * tpb

    def modspec(which):
        return pl.BlockSpec((None, None, nb, 1, d), lambda i: (layer, which, boff // nb + i // tpb, 0, 0))

    in_specs = [
        pl.BlockSpec((nb, tr, d), lambda i: (i // tpb, i % tpb, 0)),
        modspec(0), modspec(1),
        pl.BlockSpec((None, 1, d), lambda i: (layer, 0, 0)),
        pl.BlockSpec((None, d, N_EXT), lambda i: (layer, 0, 0)),
        pl.BlockSpec((tm, LANES), lambda i: (i % tpb, 0)),
        pl.BlockSpec((tm, LANES), lambda i: (i % tpb, 0)),
        pl.BlockSpec((None, 1, LANES), lambda i: (layer, 0, 0)),
        pl.BlockSpec((None, 1, LANES), lambda i: (layer, 0, 0)),
        pl.BlockSpec((LANES, LANES), lambda i: (0, 0)),
    ]
    row_outs = _IN_PROJ_ROWS + (_IN_PROJ_PROMPT_ROWS if prompt else ())
    names = [n for n, _, _ in row_outs]
    out_specs = [pl.BlockSpec((tm, w), lambda i: (i, 0)) for _, w, _ in row_outs]
    out_shape = [jax.ShapeDtypeStruct((rows, w), dt) for _, w, dt in row_outs]
    if prompt:
        assert nb == 1 and tr == KEY_BLOCK
        for n, w in _IN_PROJ_PROMPT_KEYS:
            names.append(n)
            out_specs.append(pl.BlockSpec((None, None, w, KEY_BLOCK), lambda i: (i // tpb, i % tpb, 0, 0)))
            out_shape.append(jax.ShapeDtypeStruct((b, tpb, w, KEY_BLOCK), BF16))
    outs = pl.pallas_call(
        functools.partial(_in_proj_kernel, names=tuple(names)),
        grid=(grid,), in_specs=in_specs, out_specs=out_specs, out_shape=out_shape,
        compiler_params=_params("arbitrary"), name="in_proj",
    )(x, mod, mod, norm_g, w_ext, cos, sin, qn, kn, seg)
    return dict(zip(names, outs))


def _kth_largest_key(count_ge, rows, n_sel):
    def body(it, v):
        cand = v ^ jnp.left_shift(jnp.int32(1), 31 - it)
        return jnp.where(count_ge(cand) >= n_sel, cand, v)
    return lax.fori_loop(0, 32, body, jnp.full((rows, 1), INT_MIN, I32))


def _lane_fold(m, op=jnp.add):
    acc = m[:, 0:LANES]
    for c in range(1, m.shape[1] // LANES):
        acc = op(acc, m[:, c * LANES:(c + 1) * LANES])
    return acc


def _select_top(keys_ref, nkb, rows, tk, n_sel):
    def count_ge(cand):
        def body(kb, acc):
            return acc + _lane_fold(jnp.where(keys_ref[kb] >= cand, 1, 0))
        acc = lax.fori_loop(0, nkb, body, jnp.zeros((rows, LANES), I32))
        return jnp.sum(acc, axis=1, keepdims=True)

    thr = _kth_largest_key(count_ge, rows, n_sel)
    cnt_ge = count_ge(thr)
    need = n_sel - count_ge(thr + 1)
    tied = (thr > KEY_NEG_INF) & (cnt_ge > n_sel)

    @pl.when(jnp.max(jnp.where(tied, 1, 0)) > 0)
    def _():
        r_i = lax.broadcasted_iota(I32, (tk, tk), 0)
        c_i = lax.broadcasted_iota(I32, (tk, tk), 1)
        upper = jnp.where(r_i <= c_i, 1.0, 0.0).astype(BF16)
        need_f = need.astype(F32)

        def body(kb, carry):
            k = keys_ref[kb]
            eq = k == thr
            prefix = _dot(jnp.where(eq, 1.0, 0.0).astype(BF16), upper) + carry
            keys_ref[kb] = jnp.where(eq & (prefix > need_f), thr - 1, k)
            return prefix[:, tk - 1:tk]
        lax.fori_loop(0, nkb, body, jnp.zeros((rows, 1), F32))

    return jnp.maximum(thr, KEY_NEG_INF + 1)


def _mix_a_prompt_kernel(qi_ref, wi_ref, kidT_ref, qa_ref, kadT_ref, vad_ref, o_ref,
                         keys_ref, bias_ref, bias2_ref, qm_ref, lg_ref, lg2_ref, p_ref, p2_ref, m_ref, mn_ref, acc_ref,
                         *, tk, n_sel):
    tq = qi_ref.shape[0]
    nsb = tk // KEY_BLOCK
    i = pl.program_id(1)
    nkb = ((i + 1) * tq + tk - 1) // tk
    lane = lax.broadcasted_iota(I32, (1, LANES), 1)
    half_mask = [jnp.where(lane < 64, 1.0, 0.0).astype(BF16), jnp.where(lane >= 64, 1.0, 0.0).astype(BF16)]

    def stack_heads(src_ref, dst, first):
        for r in range(A_REP):
            h = first + r
            dst[r * tq:(r + 1) * tq, :] = src_ref[:, (h // 2) * LANES:(h // 2 + 1) * LANES] * half_mask[h % 2]

    wi = wi_ref[...]
    stack_heads(qi_ref, qm_ref.at[0], 0)
    q_chunk = (i * tq + lax.broadcasted_iota(I32, (tq, 1), 0)) >> 6

    def score_block(kb, _):
        for sb in range(nsb):
            dots = _dot(qm_ref[0], kidT_ref[kb * nsb + sb])
            s = jnp.zeros((tq, KEY_BLOCK), F32)
            for h in range(IDX_HEADS):
                s = s + jnp.maximum(dots[h * tq:(h + 1) * tq, :], 0.0) * wi[:, h:h + 1]
            k_chunk = (kb * tk + sb * KEY_BLOCK + lax.broadcasted_iota(I32, (1, KEY_BLOCK), 1)) >> 6
            s = jnp.where(k_chunk <= q_chunk, s, -jnp.inf)
            keys_ref[kb, :, sb * KEY_BLOCK:(sb + 1) * KEY_BLOCK] = _sort_key(s)
        return 0
    lax.fori_loop(0, nkb, score_block, 0)

    thr = _select_top(keys_ref, nkb, tq, tk, n_sel)

    m_ref[...] = jnp.full(m_ref.shape, NEG_BIG, F32)
    acc_ref[...] = jnp.zeros(acc_ref.shape, F32)
    for g in range(A_KV_HEADS):
        stack_heads(qa_ref, qm_ref.at[g], g * A_REP)

    def masked_logits(kb, lg, bias):
        bias[...] = jnp.where(keys_ref[kb] >= thr, 0.0, MASKED)
        for g in range(A_KV_HEADS):
            for sb in range(nsb):
                cols = slice(sb * KEY_BLOCK, (sb + 1) * KEY_BLOCK)
                logits = _dot(qm_ref[g], kadT_ref[kb * nsb + sb, g * LANES:(g + 1) * LANES, :])
                for r in range(A_REP):
                    lg[g, r * tq:(r + 1) * tq, cols] = logits[r * tq:(r + 1) * tq, :] + bias[:, cols]

    def accumulate(kb, lg, p):
        start = pl.multiple_of(kb * tk, tk)
        for g in range(A_KV_HEADS):
            row_max = jnp.max(_lane_fold(lg[g], jnp.maximum), axis=1, keepdims=True)
            mn_ref[g] = jnp.maximum(m_ref[g], row_max)
            p[g] = jnp.exp2(lg[g] - mn_ref[g]).astype(BF16)
            vblk = vad_ref[pl.ds(start, tk), g * LANES:(g + 1) * LANES]
            acc_ref[g] = jnp.exp2(m_ref[g] - mn_ref[g]) * acc_ref[g] + _dot(p[g], vblk)
            m_ref[g] = mn_ref[g]

    masked_logits(0, lg_ref, bias_ref)

    def attn_pair(j, _):
        kb = 2 * j
        masked_logits(kb + 1, lg2_ref, bias2_ref)
        accumulate(kb, lg_ref, p_ref)
        masked_logits(jnp.minimum(kb + 2, nkb - 1), lg_ref, bias_ref)
        accumulate(kb + 1, lg2_ref, p2_ref)
        return 0
    lax.fori_loop(0, nkb // 2, attn_pair, 0)

    @pl.when(nkb % 2 == 1)
    def _():
        accumulate(nkb - 1, lg_ref, p_ref)

    low64 = lax.broadcasted_iota(I32, (tq, LANES), 1) < 64
    for j in range(A_HEADS // 2):
        g, r = (2 * j) // A_REP, (2 * j) % A_REP
        a0 = acc_ref[g, r * tq:(r + 1) * tq, :]
        a1 = acc_ref[g, (r + 1) * tq:(r + 2) * tq, :]
        lo = a0 / a0[:, 64:65]
        hi = pltpu.roll(a1, 64, 1) / a1[:, 64:65]
        o_ref[:, j * LANES:(j + 1) * LANES] = jnp.where(low64, lo, hi).astype(BF16)


def _mix_a_prompt(p, b, t, tq, tk):
    nq = t // tq
    n_sel = min(TOPK_MAX, t // 4)
    nkb_max = t // tk
    nsub = t // KEY_BLOCK
    qspec = lambda w: pl.BlockSpec((tq, w), lambda bb, i: (bb * nq + i, 0))
    ktspec = lambda w: pl.BlockSpec((None, nsub, w, KEY_BLOCK), lambda bb, i: (bb, 0, 0, 0))
    return pl.pallas_call(
        functools.partial(_mix_a_prompt_kernel, tk=tk, n_sel=n_sel),
        grid=(b, nq),
        in_specs=[qspec(256), qspec(WI_PAD), ktspec(128), qspec(512), ktspec(256),
                  pl.BlockSpec((t, 256), lambda bb, i: (bb, 0))],
        out_specs=qspec(512),
        out_shape=jax.ShapeDtypeStruct((b * t, 512), BF16),
        scratch_shapes=[
            pltpu.VMEM((nkb_max, tq, tk), I32),
            pltpu.VMEM((tq, tk), F32),
            pltpu.VMEM((tq, tk), F32),
            pltpu.VMEM((A_KV_HEADS, A_REP * tq, LANES), BF16),
            pltpu.VMEM((A_KV_HEADS, A_REP * tq, tk), F32),
            pltpu.VMEM((A_KV_HEADS, A_REP * tq, tk), F32),
            pltpu.VMEM((A_KV_HEADS, A_REP * tq, tk), BF16),
            pltpu.VMEM((A_KV_HEADS, A_REP * tq, tk), BF16),
            pltpu.VMEM((A_KV_HEADS, A_REP * tq, 1), F32),
            pltpu.VMEM((A_KV_HEADS, A_REP * tq, 1), F32),
            pltpu.VMEM((A_KV_HEADS, A_REP * tq, LANES), F32),
        ],
        compiler_params=_params("arbitrary", "arbitrary"),
        name="mix_a_prompt",
    )(p["qi"], p["wi"], p["kidT"], p["qa"], p["kadT"], p["vad"])


def _softplus(z):
    return jnp.maximum(z, 0.0) + jnp.log(1.0 + jnp.exp(-jnp.abs(z)))


def _mix_b_prompt_kernel(q_ref, k_ref, v_ref, tri_ref, o_ref, r_ref, acc_ref):
    tq = q_ref.shape[0]
    i = pl.program_id(2)
    lane = lax.broadcasted_iota(I32, (1, LANES), 1)
    half_mask = [jnp.where(lane < 64, 1.0, 0.0).astype(BF16), jnp.where(lane >= 64, 1.0, 0.0).astype(BF16)]
    q = q_ref[...]
    qs = jnp.concatenate([q * half_mask[0], q * half_mask[1]], axis=0)
    tri = tri_ref[...]
    r_ref[...] = jnp.zeros(r_ref.shape, F32)
    acc_ref[...] = jnp.zeros(acc_ref.shape, F32)

    def block(kb, diag):
        start = pl.multiple_of(kb * tq, tq)
        z = _dot(qs, k_ref[kb])
        sp = _softplus(z)
        log_keep = -sp
        if diag:
            local_q = lax.broadcasted_iota(I32, (2 * tq, tq), 0) & (tq - 1)
            earlier = lax.broadcasted_iota(I32, (2 * tq, tq), 1) < local_q
            log_keep = jnp.where(earlier, log_keep, 0.0)
        lh, ll = _split(log_keep)
        suffix = _dot(jnp.concatenate([lh, ll], axis=0), tri)
        between = suffix[0:2 * tq, :] + suffix[2 * tq:4 * tq, :] + r_ref[...]
        att = jnp.exp(z - sp + between)
        if diag:
            att = jnp.where(earlier, att, 0.0)
        acc_ref[...] += _dot(att.astype(BF16), v_ref[pl.ds(start, tq), :])
        r_ref[...] += jnp.sum(log_keep, axis=1, keepdims=True)

    def live():
        return jnp.where(jnp.max(r_ref[...]) >= STICK_CUTOFF, 1, 0)

    block(i, True)

    def body(c):
        block(i - 1 - c[0], False)
        return c[0] + 1, live()
    lax.while_loop(lambda c: (c[0] < i) & (c[1] > 0), body, (jnp.int32(0), live()))

    low64 = lax.broadcasted_iota(I32, (tq, LANES), 1) < 64
    o_ref[...] = jnp.where(low64, acc_ref[0:tq, :], acc_ref[tq:2 * tq, :]).astype(BF16)


def _strict_lower(n):
    j = np.arange(n)[:, None]
    s = np.arange(n)[None, :]
    return jnp.asarray((j > s).astype(np.float32), dtype=BF16)


def _mix_b_prompt(p, b, t, tq):
    assert tq == KEY_BLOCK
    nq = t // tq
    npair = B_HEADS // 2
    return pl.pallas_call(
        _mix_b_prompt_kernel,
        grid=(b, npair, nq),
        in_specs=[
            pl.BlockSpec((tq, LANES), lambda bb, j, i: (bb * nq + i, j)),
            pl.BlockSpec((None, nq, LANES, KEY_BLOCK), lambda bb, j, i: (bb, 0, j, 0)),
            pl.BlockSpec((t, LANES), lambda bb, j, i: (bb, j)),
            pl.BlockSpec((tq, tq), lambda bb, j, i: (0, 0)),
        ],
        out_specs=pl.BlockSpec((tq, LANES), lambda bb, j, i: (bb * nq + i, j)),
        out_shape=jax.ShapeDtypeStruct((b * t, B_HEADS * HEAD_DIM), BF16),
        scratch_shapes=[pltpu.VMEM((2 * tq, 1), F32), pltpu.VMEM((2 * tq, LANES), F32)],
        compiler_params=_params("arbitrary", "arbitrary", "arbitrary"),
        name="mix_b_prompt",
    )(p["qb"], p["kbT"], p["vbh"], _strict_lower(tq))


def _mix_a_sample_kernel(qi_ref, wi_ref, cki_ref, nki_ref, qa_ref, cka_ref, cva_ref, nka_ref, nva_ref, o_ref,
                         keys_ref, kpad_ref, m_ref, l_ref, acc_ref, *, tk, n_sel):
    t = wi_ref.shape[0]
    past = cki_ref.shape[0]
    nkb = past // tk
    rows = A_REP * t

    wi = wi_ref[...]
    qi = qi_ref[...]
    col = lax.broadcasted_iota(I32, (t, tk), 1)

    def scores(kblk):
        d = _dot_nt(qi, kblk)
        s = jnp.zeros((t, tk), F32)
        for h in range(IDX_HEADS):
            s = s + jnp.maximum(d[h * t:(h + 1) * t, :], 0.0) * wi[:, h:h + 1]
        return s

    def score_block(kb, _):
        start = pl.multiple_of(kb * tk, tk)
        keys_ref[kb] = _sort_key(scores(cki_ref[pl.ds(start, tk), :].astype(BF16)))
        return 0
    lax.fori_loop(0, nkb, score_block, 0)
    kpad_ref[...] = jnp.zeros(kpad_ref.shape, F32)
    kpad_ref[0:t, 0:IDX_DIM] = nki_ref[...]
    s_new = scores(kpad_ref[:, 0:IDX_DIM].astype(BF16))
    q_chunk = (past + lax.broadcasted_iota(I32, (t, 1), 0)) >> 6
    admissible = (col < t) & (((past + col) >> 6) <= q_chunk)
    keys_ref[nkb] = _sort_key(jnp.where(admissible, s_new, -jnp.inf))

    thr = _select_top(keys_ref, nkb + 1, t, tk, n_sel)

    m_ref[...] = jnp.full(m_ref.shape, NEG_BIG, F32)
    l_ref[...] = jnp.zeros(l_ref.shape, F32)
    acc_ref[...] = jnp.zeros(acc_ref.shape, F32)

    thr_rep = jnp.concatenate([thr] * A_REP, axis=0)

    def attend(kb, kblk, vblk):
        sel = jnp.concatenate([keys_ref[kb]] * A_REP, axis=0) >= thr_rep
        for g in range(A_KV_HEADS):
            logits = jnp.where(sel, _dot_nt(qa_ref[g], kblk), NEG_BIG)
            m_old = m_ref[g]
            m_new = jnp.maximum(m_old, jnp.max(logits, axis=1, keepdims=True))
            alpha = jnp.exp2(m_old - m_new)
            p = jnp.where(sel, jnp.exp2(logits - m_new), 0.0)
            l_ref[g] = alpha * l_ref[g] + jnp.sum(p, axis=1, keepdims=True)
            acc_ref[g] = alpha * acc_ref[g] + _dot(p.astype(BF16), vblk)
            m_ref[g] = m_new

    def attn_block(kb, _):
        start = pl.multiple_of(kb * tk, tk)
        attend(kb, cka_ref[pl.ds(start, tk), :].astype(BF16), cva_ref[pl.ds(start, tk), :].astype(BF16))
        return 0
    lax.fori_loop(0, nkb, attn_block, 0)
    kpad_ref[0:t, :] = nka_ref[...]
    knew = kpad_ref[...].astype(BF16)
    kpad_ref[0:t, :] = nva_ref[...]
    vnew = kpad_ref[...].astype(BF16)
    attend(nkb, knew, vnew)

    for g in range(A_KV_HEADS):
        o_ref[g] = acc_ref[g] / l_ref[g]


def _mix_a_sample(qi_hm, wi, cache_idx_k, nik, qa_g, cache_a_k, cache_a_v, nak, nav, layer, tk):
    b, t, _ = wi.shape
    past = cache_idx_k.shape[2]
    n_sel = min(TOPK_MAX, (past + t) // 4)
    kvw = A_KV_HEADS * HEAD_DIM
    rows = A_REP * t
    per_b = lambda *blk: pl.BlockSpec((None,) + blk, lambda bb: (bb,) + (0,) * len(blk))
    cache = lambda w: pl.BlockSpec((None, None, past, w), lambda bb: (layer, bb, 0, 0))
    return pl.pallas_call(
        functools.partial(_mix_a_sample_kernel, tk=tk, n_sel=n_sel),
        grid=(b,),
        in_specs=[per_b(IDX_HEADS * t, IDX_DIM), per_b(t, WI_PAD), cache(IDX_DIM), per_b(t, IDX_DIM),
                  per_b(A_KV_HEADS, rows, kvw), cache(kvw), cache(kvw), per_b(t, kvw), per_b(t, kvw)],
        out_specs=per_b(A_KV_HEADS, rows, kvw),
        out_shape=jax.ShapeDtypeStruct((b, A_KV_HEADS, rows, kvw), F32),
        scratch_shapes=[
            pltpu.VMEM((past // tk + 1, t, tk), I32),
            pltpu.VMEM((tk, kvw), F32),
            pltpu.VMEM((A_KV_HEADS, rows, 1), F32),
            pltpu.VMEM((A_KV_HEADS, rows, 1), F32),
            pltpu.VMEM((A_KV_HEADS, rows, kvw), F32),
        ],
        compiler_params=_params("arbitrary"),
        name="mix_a_sample",
    )(qi_hm, wi, cache_idx_k, nik, qa_g, cache_a_k.reshape(cache_a_k.shape[:3] + (kvw,)),
      cache_a_v.reshape(cache_a_v.shape[:3] + (kvw,)), nak, nav)


def _mix_b_sample_kernel(wq_ref, ck_ref, cv_ref, nk_ref, nv_ref, tri_ref, o_ref, r_ref, acc_ref, pad_ref, live_ref,
                         *, t):
    n = pl.program_id(1)
    tk = ck_ref.shape[0]
    wq = wq_ref[...]
    tri = tri_ref[...]

    def block(kblk, vblk, earlier):
        z = _dot(kblk, wq)
        sp = _softplus(z)
        log_keep = -sp
        if earlier is not None:
            log_keep = jnp.where(earlier, log_keep, 0.0)
        lh, ll = _split(log_keep)
        between = _dot(tri, lh) + _dot(tri, ll) + r_ref[...]
        att = jnp.exp(z - sp + between)
        if earlier is not None:
            att = jnp.where(earlier, att, 0.0)
        acc_ref[...] += _dot(att.T.astype(BF16), vblk)
        r_new = r_ref[...] + jnp.sum(log_keep, axis=0, keepdims=True)
        r_ref[...] = r_new
        live_ref[0] = jnp.where(jnp.max(r_new) >= STICK_CUTOFF, 1, 0)

    @pl.when(n == 0)
    def _():
        r_ref[...] = jnp.zeros(r_ref.shape, F32)
        acc_ref[...] = jnp.zeros(acc_ref.shape, F32)
        key_j = lax.broadcasted_iota(I32, (tk, LANES), 0)
        query_i = lax.broadcasted_iota(I32, (tk, LANES), 1) & (t - 1)
        pad_ref[...] = jnp.zeros(pad_ref.shape, F32)
        pad_ref[0:t, :] = nk_ref[...]
        knew = pad_ref[...].astype(BF16)
        pad_ref[0:t, :] = nv_ref[...]
        vnew = pad_ref[...].astype(BF16)
        block(knew, vnew, key_j < query_i)

    @pl.when((n > 0) & (live_ref[0] > 0))
    def _():
        block(ck_ref[...].astype(BF16), cv_ref[...].astype(BF16), None)

    @pl.when(n == pl.num_programs(1) - 1)
    def _():
        width = acc_ref.shape[1]
        head_of_lane = lax.broadcasted_iota(I32, (t, width), 1) >> 6
        out = jnp.zeros((t, width), F32)
        for h in range(B_HEADS):
            out = out + jnp.where(head_of_lane == h, acc_ref[h * t:(h + 1) * t, :], 0.0)
        o_ref[...] = out


def _mix_b_sample(wq, cache_b_k, cache_b_v, nbk, nbv, layer, tk):
    b, t, width = nbk.shape
    assert t & (t - 1) == 0 and B_HEADS * t == LANES
    past = cache_b_k.shape[2]
    nblk = past // tk
    cache = pl.BlockSpec((None, None, tk, width), lambda bb, n: (layer, bb, nblk - jnp.maximum(n, 1), 0))
    per_b = lambda r, w: pl.BlockSpec((None, r, w), lambda bb, n: (bb, 0, 0))
    tri = jnp.asarray((np.arange(tk)[None, :] > np.arange(tk)[:, None]).astype(np.float32), dtype=BF16)
    return pl.pallas_call(
        functools.partial(_mix_b_sample_kernel, t=t),
        grid=(b, nblk + 1),
        in_specs=[per_b(width, LANES), cache, cache, per_b(t, width), per_b(t, width),
                  pl.BlockSpec((tk, tk), lambda bb, n: (0, 0))],
        out_specs=per_b(t, width),
        out_shape=jax.ShapeDtypeStruct((b, t, width), F32),
        scratch_shapes=[pltpu.VMEM((1, LANES), F32), pltpu.VMEM((LANES, width), F32), pltpu.VMEM((tk, width), F32),
                        pltpu.SMEM((1,), I32)],
        compiler_params=_params("arbitrary", "arbitrary"),
        name="mix_b_sample",
    )(wq, cache_b_k.reshape(cache_b_k.shape[:3] + (width,)), cache_b_v.reshape(cache_b_v.shape[:3] + (width,)),
      nbk, nbv, tri)


def _merge_kernel(x_ref, oa_ref, ob_ref, gate_ref, g1_ref, sh2_ref, sc2_ref, n2_ref,
                  wa_ref, wb_ref, wo_ref, wr_ref, br_ref, x1_o, h2_o, rg_o):
    nb, tr, d = x_ref.shape
    tm = nb * tr
    a = _dot(oa_ref[...], wa_ref[...])
    bb = _dot(ob_ref[...], wb_ref[...])
    merged = gate_ref[:, 0:d] * a + gate_ref[:, d:2 * d] * bb
    mix = _dot(merged.astype(BF16), wo_ref[...])
    x1 = x_ref[...] + g1_ref[...] * mix.reshape(nb, tr, d)
    x1_o[...] = x1
    ms = jnp.mean(x1 * x1, axis=-1, keepdims=True)
    h2 = ((x1 * lax.rsqrt(ms + RMS_EPS) * n2_ref[...]) * (1.0 + sc2_ref[...]) + sh2_ref[...]).reshape(tm, d)
    h2_o[...] = h2.astype(BF16)

    logits = _dot3(h2, wr_ref[...])
    e = jnp.exp(logits - jnp.max(logits, axis=1, keepdims=True))
    aff = e / jnp.sum(e, axis=1, keepdims=True)
    biased = aff + br_ref[...]
    col = [biased[:, j:j + 1] for j in range(N_EXPERTS)]
    best_val = None
    for g in range(N_GROUPS):
        c = col[g * EXP_PER_GROUP:(g + 1) * EXP_PER_GROUP]
        score = None
        for u in range(EXP_PER_GROUP):
            for w in range(u + 1, EXP_PER_GROUP):
                pair = c[u] + c[w]
                score = pair if score is None else jnp.maximum(score, pair)
        if best_val is None:
            best_val, best_g = score, jnp.zeros((tm, 1), I32)
        else:
            upd = score > best_val
            best_g = jnp.where(upd, g, best_g)
            best_val = jnp.where(upd, score, best_val)
    eidx = lax.broadcasted_iota(I32, (tm, N_EXPERTS), 1)
    cand = jnp.where((eidx >> 2) == best_g, biased, -jnp.inf)
    m1 = jnp.max(cand, axis=1, keepdims=True)
    i1 = jnp.min(jnp.where(cand == m1, eidx, N_EXPERTS), axis=1, keepdims=True)
    cand2 = jnp.where(eidx == i1, -jnp.inf, cand)
    m2 = jnp.max(cand2, axis=1, keepdims=True)
    i2 = jnp.min(jnp.where(cand2 == m2, eidx, N_EXPERTS), axis=1, keepdims=True)
    w1 = jnp.sum(jnp.where(eidx == i1, aff, 0.0), axis=1, keepdims=True)
    w2 = jnp.sum(jnp.where(eidx == i2, aff, 0.0), axis=1, keepdims=True)
    tot = w1 + w2
    rg_o[...] = jnp.where(eidx == i1, w1 / tot, 0.0) + jnp.where(eidx == i2, w2 / tot, 0.0)


def _merge(x, oa, ob, gates, mod, layer, boff, n2, wa, wb, wo, wr, br, nb, tr):
    b, t, d = x.shape
    tpb = t // tr
    tm = nb * tr
    rows = b * t
    grid = (b // nb) * tpb

    def modspec(which):
        return pl.BlockSpec((None, None, nb, 1, d), lambda i: (layer, which, boff // nb + i // tpb, 0, 0))

    xspec = pl.BlockSpec((nb, tr, d), lambda i: (i // tpb, i % tpb, 0))
    row = lambda w: pl.BlockSpec((tm, w), lambda i: (i, 0))
    lw = lambda r, c: pl.BlockSpec((None, r, c), lambda i: (layer, 0, 0))
    return pl.pallas_call(
        _merge_kernel, grid=(grid,),
        in_specs=[xspec, row(512), row(512), row(2 * d), modspec(2), modspec(3), modspec(4), lw(1, d),
                  lw(512, d), lw(512, d), lw(d, d),
                  pl.BlockSpec((d, N_EXPERTS), lambda i: (0, 0)), pl.BlockSpec((1, N_EXPERTS), lambda i: (0, 0))],
        out_specs=[xspec, row(d), row(N_EXPERTS)],
        out_shape=[jax.ShapeDtypeStruct((b, t, d), F32), jax.ShapeDtypeStruct((rows, d), BF16),
                   jax.ShapeDtypeStruct((rows, N_EXPERTS), F32)],
        compiler_params=_params("arbitrary"), name="merge",
    )(x, oa, ob, gates, mod, mod, mod, n2, wa, wb, wo, wr, br)


def _moe_kernel(x_ref, h_ref, rg_ref, g2_ref, wg_ref, wu_ref, wd_ref, o_ref, acc_ref):
    nb, tr, d = x_ref.shape
    tm = nb * tr
    e = pl.program_id(1)

    @pl.when(e == 0)
    def _():
        acc_ref[...] = jnp.zeros(acc_ref.shape, F32)

    h = h_ref[...]
    gate = _dot(h, wg_ref[...])
    up = _dot(h, wu_ref[...])
    hidden = (gate * jax.nn.sigmoid(gate)) * up
    y = _dot(hidden.astype(BF16), wd_ref[...])
    eidx = lax.broadcasted_iota(I32, (tm, N_EXPERTS), 1)
    w = jnp.sum(jnp.where(eidx == e, rg_ref[...], 0.0), axis=1, keepdims=True)
    acc_ref[...] += w * y

    @pl.when(e == N_EXPERTS - 1)
    def _():
        o_ref[...] = x_ref[...] + g2_ref[...] * acc_ref[...].reshape(nb, tr, d)


def _moe(x1, h2, rg, mod, layer, boff, wg, wu, wd, nb, tr):
    b, t, d = x1.shape
    tpb = t // tr
    tm = nb * tr
    grid = (b // nb) * tpb
    de = wg.shape[-1]
    xspec = pl.BlockSpec((nb, tr, d), lambda i, e: (i // tpb, i % tpb, 0))
    return pl.pallas_call(
        _moe_kernel, grid=(grid, N_EXPERTS),
        in_specs=[xspec, pl.BlockSpec((tm, d), lambda i, e: (i, 0)), pl.BlockSpec((tm, N_EXPERTS), lambda i, e: (i, 0)),
                  pl.BlockSpec((None, None, nb, 1, d), lambda i, e: (layer, 5, boff // nb + i // tpb, 0, 0)),
                  pl.BlockSpec((None, None, d, de), lambda i, e: (layer, e, 0, 0)),
                  pl.BlockSpec((None, None, d, de), lambda i, e: (layer, e, 0, 0)),
                  pl.BlockSpec((None, None, de, d), lambda i, e: (layer, e, 0, 0))],
        out_specs=xspec,
        out_shape=jax.ShapeDtypeStruct((b, t, d), F32),
        scratch_shapes=[pltpu.VMEM((tm, d), F32)],
        compiler_params=_params("arbitrary", "arbitrary"), name="moe",
    )(x1, h2, rg, mod, wg, wu, wd)


def _rope_tables(pos):
    inv_freq = ROPE_THETA ** (-jnp.arange(0, HEAD_DIM, 2, dtype=F32) / HEAD_DIM)
    ang = pos.astype(F32)[:, None] * inv_freq[None, :]
    c, s = jnp.cos(ang), jnp.sin(ang)
    cos = jnp.concatenate([c, c, c, c], axis=1)
    sin = jnp.concatenate([-s, s, -s, s], axis=1)
    return cos, sin


def _widen_w_in(w_in):
    sizes = (512, 128, 128, 256, 64, 4, 512, 512, 512, 2048)
    pts = [int(v) for v in np.cumsum(sizes)[:-1]]
    qa, ka, va, qi, ki, wi, qb, kb, vb, gl = jnp.split(w_in, pts, axis=-1)
    dup = lambda w: jnp.concatenate([w[..., 0:64], w[..., 0:64], w[..., 64:128], w[..., 64:128]], axis=-1)
    wi_pad = jnp.pad(wi, ((0, 0), (0, 0), (0, LANES - wi.shape[-1])))
    ext = jnp.concatenate([qa, dup(ka), dup(va), qi, ki, ki, wi_pad, qb, kb, vb, gl], axis=-1)
    assert ext.shape[-1] == N_EXT
    return ext.astype(BF16)


def kernel(x_prompt, x_sample, cache_a_k, cache_a_v, cache_idx_k, cache_b_k, cache_b_v, c_prompt, c_sample,
           norm1_g, norm2_g, w_ada, b_ada, w_in, qn_g, kn_g, w_branch_a, w_branch_b, w_out, w_router, b_router,
           w_exp_gate, w_exp_up, w_exp_down):
    depth = w_in.shape[0]
    bp, tp, d = x_prompt.shape
    bs, ts, _ = x_sample.shape
    past = cache_a_k.shape[2]

    nc = -(-(bs + bp) // 8) * 8
    c_all = jnp.concatenate([c_sample, c_prompt, jnp.zeros((nc - bs - bp, d), F32)], axis=0)
    mod = _ada(c_all, w_ada, b_ada).reshape(depth, 6, nc, 1, d)
    boff_s, boff_p = 0, bs

    w_ext = _widen_w_in(w_in)
    wa, wb, wo = w_branch_a.astype(BF16), w_branch_b.astype(BF16), w_out.astype(BF16)
    wg, wu, wd = w_exp_gate.astype(BF16), w_exp_up.astype(BF16), w_exp_down.astype(BF16)
    n1 = norm1_g.reshape(depth, 1, d)
    n2 = norm2_g.reshape(depth, 1, d)
    qn = jnp.tile(qn_g, (1, 2)).reshape(depth, 1, LANES)
    kn = jnp.tile(kn_g, (1, 2)).reshape(depth, 1, LANES)
    seg = jnp.asarray(np.kron(np.eye(2), np.ones((64, 64))).astype(np.float32), dtype=BF16)
    br = b_router.reshape(1, N_EXPERTS)

    cos_p, sin_p = _rope_tables(jnp.arange(tp, dtype=I32))
    cos_s, sin_s = _rope_tables(past + jnp.arange(ts, dtype=I32))
    cos_s, sin_s = jnp.tile(cos_s, (bs, 1)), jnp.tile(sin_s, (bs, 1))

    tr_p = min(256, tp)
    tq_a = min(256, tp)
    tk_a = min(512, tp)
    tq_b = min(256, tp)
    tm_moe = min(1024, tp)
    tk_s = min(512, past)

    xp, xs = x_prompt, x_sample
    rows_p, rows_s = [], []
    for l in range(depth):
        p = _in_proj(xp, mod, l, boff_p, n1, w_ext, cos_p, sin_p, qn, kn, seg, 1, tr_p, True)
        oa = _mix_a_prompt(p, bp, tp, tq_a, tk_a)
        ob = _mix_b_prompt(p, bp, tp, tq_b)
        x1, h2, rg = _merge(xp, oa, ob, p["gates"], mod, l, boff_p, n2, wa, wb, wo, w_router, br, 1, tr_p)
        xp = _moe(x1, h2, rg, mod, l, boff_p, wg, wu, wd, 1, tm_moe)
        rows_p.append(p)

        s = _in_proj(xs, mod, l, boff_s, n1, w_ext, cos_s, sin_s, qn, kn, seg, bs, ts, False)
        qi_hm = s["qi"].reshape(bs, ts, IDX_HEADS, IDX_DIM).transpose(0, 2, 1, 3).reshape(bs, IDX_HEADS * ts, IDX_DIM)
        qa5 = s["qa"].reshape(bs, ts, A_KV_HEADS, A_REP, HEAD_DIM).transpose(0, 2, 3, 1, 4)
        qa5 = qa5.reshape(bs, A_KV_HEADS, A_REP * ts, HEAD_DIM)
        zeros = jnp.zeros_like(qa5[:, 0])
        qa_g = jnp.stack([jnp.concatenate([qa5[:, 0], zeros], axis=-1),
                          jnp.concatenate([zeros, qa5[:, 1]], axis=-1)], axis=1)
        oa_s = _mix_a_sample(qi_hm, s["wi"].reshape(bs, ts, WI_PAD), cache_idx_k, s["nik"].reshape(bs, ts, IDX_DIM),
                             qa_g, cache_a_k, cache_a_v, s["nak"].reshape(bs, ts, 128), s["nav"].reshape(bs, ts, 128),
                             l, tk_s)
        oa_s = jnp.stack([oa_s[:, g, :, g * HEAD_DIM:(g + 1) * HEAD_DIM] for g in range(A_KV_HEADS)], axis=1)
        oa_s = oa_s.reshape(bs, A_KV_HEADS, A_REP, ts, HEAD_DIM).transpose(0, 3, 1, 2, 4).reshape(bs * ts, 512)
        qb4 = s["qb"].reshape(bs, ts, B_HEADS, HEAD_DIM).transpose(0, 2, 3, 1)
        eye = jnp.eye(B_HEADS, dtype=BF16)
        wq = (qb4[:, :, :, None, :] * eye[None, :, None, :, None]).reshape(bs, B_HEADS * HEAD_DIM, B_HEADS * ts)
        ob_s = _mix_b_sample(wq, cache_b_k, cache_b_v, s["nbk"].reshape(bs, ts, 512), s["nbv"].reshape(bs, ts, 512),
                             l, tk_s)
        x1s, h2s, rgs = _merge(xs, oa_s.astype(BF16), ob_s.reshape(bs * ts, 512).astype(BF16), s["gates"], mod, l,
                               boff_s, n2, wa, wb, wo, w_router, br, bs, ts)
        xs = _moe(x1s, h2s, rgs, mod, l, boff_s, wg, wu, wd, bs, ts)
        rows_s.append(s)

    def stack(rows, name, b, t, tail):
        return jnp.stack([r[name].reshape((b, t) + tail) for r in rows], axis=0)

    kv = (A_KV_HEADS, HEAD_DIM)
    bh = (B_HEADS, HEAD_DIM)
    return (xp, xs,
            stack(rows_p, "nak", bp, tp, kv), stack(rows_p, "nav", bp, tp, kv), stack(rows_p, "nik", bp, tp, (IDX_DIM,)),
            stack(rows_p, "nbk", bp, tp, bh), stack(rows_p, "nbv", bp, tp, bh),
            stack(rows_s, "nak", bs, ts, kv), stack(rows_s, "nav", bs, ts, kv), stack(rows_s, "nik", bs, ts, (IDX_DIM,)),
            stack(rows_s, "nbk", bs, ts, bh), stack(rows_s, "nbv", bs, ts, bh))
```

```python
import functools

import jax
import jax.numpy as jnp
import numpy as np
from jax import lax
from jax.experimental import pallas as pl
from jax.experimental.pallas import tpu as pltpu

F32 = jnp.float32
BF16 = jnp.bfloat16
I32 = jnp.int32

HEAD_DIM = 64
A_HEADS = 8
A_KV_HEADS = 2
A_REP = A_HEADS // A_KV_HEADS
IDX_HEADS = 4
IDX_DIM = 64
B_HEADS = 8
CHUNK = 64
TOPK_MAX = 256
ROPE_THETA = 10000.0
N_EXPERTS = 16
N_GROUPS = 4
EXP_PER_GROUP = N_EXPERTS // N_GROUPS
RMS_EPS = 1e-6
LANES = 128
LOG2_E = 1.4426950408889634
A_Q_SCALE = HEAD_DIM ** -0.5 * LOG2_E
NEG_BIG = -1e30
MASKED = -3e30
STICK_CUTOFF = -120.0
INT_MIN = -2147483648
KEY_NEG_INF = -2139095041
MIN_NORMAL_BITS = 0x00800000
MIN_NORMAL = 2.0 ** -126
FLOAT_LOWEST = -3.4028234663852886e38

C_QA = 0
C_KAD = 512
C_VAD = 768
C_QI = 1024
C_KID = 1280
C_WI = 1408
C_QB = 1536
C_KB = 2048
C_VB = 2560
C_GL = 3072
N_EXT = 5120
WI_PAD = 8

VMEM_LIMIT = 56 * 1024 * 1024


def _dot(a, b):
    return jnp.dot(a, b, preferred_element_type=F32)


def _dot_nt(a, b):
    return lax.dot_general(a, b, (((1,), (1,)), ((), ())), preferred_element_type=F32)


def _split(a):
    hi = a.astype(BF16)
    lo = (a - hi.astype(F32)).astype(BF16)
    return hi, lo


def _dot3(a, b):
    ah, al = _split(a)
    bh, bl = _split(b)
    return _dot(ah, bh) + _dot(al, bh) + _dot(ah, bl)


def _params(*sem):
    return pltpu.CompilerParams(dimension_semantics=sem, vmem_limit_bytes=VMEM_LIMIT)


def _ada_kernel(c_ref, w_ref, b_ref, o_ref):
    c = c_ref[...]
    s = c * jax.nn.sigmoid(c)
    o_ref[...] = _dot3(s, w_ref[...]) + b_ref[...]


def _ada(c_all, w_ada, b_ada):
    depth, d, _ = w_ada.shape
    nc = c_all.shape[0]
    return pl.pallas_call(
        _ada_kernel,
        grid=(depth, 6),
        in_specs=[
            pl.BlockSpec((nc, d), lambda l, j: (0, 0)),
            pl.BlockSpec((None, d, d), lambda l, j: (l, 0, j)),
            pl.BlockSpec((None, 1, d), lambda l, j: (l, 0, j)),
        ],
        out_specs=pl.BlockSpec((None, None, nc, d), lambda l, j: (l, j, 0, 0)),
        out_shape=jax.ShapeDtypeStruct((depth, 6, nc, d), F32),
        compiler_params=_params("arbitrary", "arbitrary"),
        name="ada_mod",
    )(c_all, w_ada, b_ada.reshape(depth, 1, 6 * d))


def _rope(v, cos, sin, first_half):
    partner = jnp.where(first_half, pltpu.roll(v, LANES - 32, 1), pltpu.roll(v, 32, 1))
    return v * cos + partner * sin


_IN_PROJ_ROWS = (("qa", 512, BF16), ("qi", 256, BF16), ("wi", WI_PAD, F32), ("qb", 512, BF16), ("gates", 2048, F32))
_IN_PROJ_NEW_ROWS = (("nak", 128), ("nav", 128), ("nik", 64), ("nbk", 512), ("nbv", 512))
_IN_PROJ_PROMPT_ROWS = (("vad", 256, BF16), ("vbh", 512, BF16))
_IN_PROJ_PROMPT_KEYS = (("kadT", 256), ("kidT", 128), ("kbT", 512))
KEY_BLOCK = 256


def _in_proj_kernel(x_ref, sh_ref, sc_ref, g_ref, w_ref, cos_ref, sin_ref, qn_ref, kn_ref, seg_ref, *out_refs,
                    names):
    o = dict(zip(names, out_refs))
    prompt = "kadT" in o
    nb, tr, d = x_ref.shape
    tm = nb * tr
    x = x_ref[...]
    ms = jnp.mean(x * x, axis=-1, keepdims=True)
    h = (x * lax.rsqrt(ms + RMS_EPS) * g_ref[...]) * (1.0 + sc_ref[...]) + sh_ref[...]
    hb = h.reshape(tm, d).astype(BF16)

    def proj(lo, width):
        return _dot(hb, w_ref[:, lo:lo + width])

    cos = cos_ref[...]
    sin = sin_ref[...]
    seg = seg_ref[...]
    lane = lax.broadcasted_iota(I32, (tm, LANES), 1)
    first_half = (lane & 32) == 0
    low64 = lane < 64

    def head_norm(v, gain):
        v2 = v * v
        v2h, v2l = _split(v2)
        msq = (_dot(v2h, seg) + _dot(v2l, seg)) * (1.0 / HEAD_DIM)
        return v * lax.rsqrt(msq + RMS_EPS) * gain

    qn = qn_ref[...]
    kn = kn_ref[...]
    v = proj(C_QA, 512)
    for c in range(4):
        blk = _rope(head_norm(v[:, c * LANES:(c + 1) * LANES], qn), cos, sin, first_half)
        o["qa"][:, c * LANES:(c + 1) * LANES] = (blk * A_Q_SCALE).astype(BF16)
    v = proj(C_KAD, 256)
    kd = [_rope(head_norm(v[:, c * LANES:(c + 1) * LANES], kn), cos, sin, first_half) for c in range(2)]
    v = proj(C_VAD, 256)
    if prompt:
        for c in range(2):
            kT = kd[c].T
            o["kadT"][c * LANES:(c + 1) * LANES, :] = kT.astype(BF16)
            o["nak"][c * HEAD_DIM:(c + 1) * HEAD_DIM, :] = kT[0:HEAD_DIM, :]
            o["nav"][c * HEAD_DIM:(c + 1) * HEAD_DIM, :] = v[:, c * LANES:(c + 1) * LANES].T[0:HEAD_DIM, :]
        ones_col = jnp.where(lane == 64, 1.0, 0.0)
        for c in range(2):
            blk = jnp.where(low64, v[:, c * LANES:(c + 1) * LANES], ones_col)
            o["vad"][:, c * LANES:(c + 1) * LANES] = blk.astype(BF16)
    else:
        o["nak"][...] = jnp.where(low64, kd[0], kd[1])
        o["nav"][...] = jnp.where(low64, v[:, 0:LANES], v[:, LANES:2 * LANES])
    v = proj(C_QI, 256)
    for c in range(2):
        blk = _rope(v[:, c * LANES:(c + 1) * LANES], cos, sin, first_half)
        o["qi"][:, c * LANES:(c + 1) * LANES] = (blk * (IDX_DIM ** -0.5)).astype(BF16)
    v = _rope(proj(C_KID, LANES), cos, sin, first_half)
    if prompt:
        kT = v.T
        o["kidT"][...] = kT.astype(BF16)
        o["nik"][...] = kT[0:IDX_DIM, :]
    else:
        o["nik"][...] = v[:, 0:IDX_DIM]
    v = proj(C_WI, LANES)
    o["wi"][...] = v[:, 0:WI_PAD] * (IDX_HEADS ** -0.5)
    o["qb"][...] = (proj(C_QB, 512) * (HEAD_DIM ** -0.5)).astype(BF16)
    v = proj(C_KB, 512)
    if prompt:
        kT = v.T
        o["kbT"][...] = kT.astype(BF16)
        o["nbk"][...] = kT
    else:
        o["nbk"][...] = v
    v = proj(C_VB, 512)
    if prompt:
        o["vbh"][...] = v.astype(BF16)
        o["nbv"][...] = v.T
    else:
        o["nbv"][...] = v
    for c in range(4):
        o["gates"][:, c * 512:(c + 1) * 512] = jax.nn.sigmoid(proj(C_GL + c * 512, 512))


def _in_proj(x, mod, layer, boff, norm_g, w_ext, cos, sin, qn, kn, seg, nb, tr, prompt):
    b, t, d = x.shape
    tpb = t // tr
    tm = nb * tr
    rows = b * t
    grid = (b // nb) * tpb

    def modspec(which):
        return pl.BlockSpec((None, None, nb, 1, d), lambda i: (layer, which, boff // nb + i // tpb, 0, 0))

    in_specs = [
        pl.BlockSpec((nb, tr, d), lambda i: (i // tpb, i % tpb, 0)),
        modspec(0), modspec(1),
        pl.BlockSpec((None, 1, d), lambda i: (layer, 0, 0)),
        pl.BlockSpec((None, d, N_EXT), lambda i: (layer, 0, 0)),
        pl.BlockSpec((tm, LANES), lambda i: (i % tpb, 0)),
        pl.BlockSpec((tm, LANES), lambda i: (i % tpb, 0)),
        pl.BlockSpec((None, 1, LANES), lambda i: (layer, 0, 0)),
        pl.BlockSpec((None, 1, LANES), lambda i: (layer, 0, 0)),
        pl.BlockSpec((LANES, LANES), lambda i: (0, 0)),
    ]
    row_outs = _IN_PROJ_ROWS + (_IN_PROJ_PROMPT_ROWS if prompt else tuple((n, w, F32) for n, w in _IN_PROJ_NEW_ROWS))
    names = [n for n, _, _ in row_outs]
    out_specs = [pl.BlockSpec((tm, w), lambda i: (i, 0)) for _, w, _ in row_outs]
    out_shape = [jax.ShapeDtypeStruct((rows, w), dt) for _, w, dt in row_outs]
    if prompt:
        assert nb == 1 and tr == KEY_BLOCK
        for n, w in _IN_PROJ_PROMPT_KEYS:
            names.append(n)
            out_specs.append(pl.BlockSpec((None, None, w, KEY_BLOCK), lambda i: (i // tpb, i % tpb, 0, 0)))
            out_shape.append(jax.ShapeDtypeStruct((b, tpb, w, KEY_BLOCK), BF16))
        for n, w in _IN_PROJ_NEW_ROWS:
            names.append(n)
            out_specs.append(pl.BlockSpec((None, w, KEY_BLOCK), lambda i: (i // tpb, 0, i % tpb)))
            out_shape.append(jax.ShapeDtypeStruct((b, w, t), F32))
    outs = pl.pallas_call(
        functools.partial(_in_proj_kernel, names=tuple(names)),
        grid=(grid,), in_specs=in_specs, out_specs=out_specs, out_shape=out_shape,
        compiler_params=_params("arbitrary"), name="in_proj",
    )(x, mod, mod, norm_g, w_ext, cos, sin, qn, kn, seg)
    return dict(zip(names, outs))


def _candidate_float(key):
    bits = key ^ ((key >> 31) & 0x7FFFFFFF)
    bits = jnp.where((key > 0) & (key < MIN_NORMAL_BITS), MIN_NORMAL_BITS, bits)
    return lax.bitcast_convert_type(bits, F32)


def _lane_fold(m, op=jnp.add):
    acc = m[:, 0:LANES]
    for c in range(1, m.shape[1] // LANES):
        acc = op(acc, m[:, c * LANES:(c + 1) * LANES])
    return acc


def _select_top(sc_ref, nkb, rows, tk, n_sel):
    def count(compare, c):
        def body(kb, acc):
            return acc + _lane_fold(jnp.where(compare(sc_ref[kb], c), 1, 0))
        acc = lax.fori_loop(0, nkb, body, jnp.zeros((rows, LANES), I32))
        return jnp.sum(acc, axis=1, keepdims=True)

    def body(it, v):
        cand = v ^ jnp.left_shift(jnp.int32(1), 31 - it)
        return jnp.where(count(jnp.greater_equal, _candidate_float(cand)) >= n_sel, cand, v)
    v = lax.fori_loop(0, 32, body, jnp.full((rows, 1), INT_MIN, I32))
    thr = jnp.where(v < KEY_NEG_INF, -jnp.inf, _candidate_float(v))
    cnt_ge = count(jnp.greater_equal, thr)
    need = n_sel - count(jnp.greater, thr)
    tied = (thr > -jnp.inf) & (cnt_ge > n_sel)

    @pl.when(jnp.max(jnp.where(tied, 1, 0)) > 0)
    def _():
        r_i = lax.broadcasted_iota(I32, (tk, tk), 0)
        c_i = lax.broadcasted_iota(I32, (tk, tk), 1)
        upper = jnp.where(r_i <= c_i, 1.0, 0.0).astype(BF16)
        need_f = need.astype(F32)

        def body(kb, carry):
            s = sc_ref[kb]
            eq = s == thr
            prefix = _dot(jnp.where(eq, 1.0, 0.0).astype(BF16), upper) + carry
            sc_ref[kb] = jnp.where(eq & (prefix > need_f), -jnp.inf, s)
            return prefix[:, tk - 1:tk]
        lax.fori_loop(0, nkb, body, jnp.zeros((rows, 1), F32))

    return jnp.maximum(thr, FLOAT_LOWEST)


def _mix_a_prompt_kernel(qi_ref, wi_ref, kidT_ref, qa_ref, kadT_ref, vad_ref, o_ref,
                         sc_ref, bias_ref, bias2_ref, qm_ref, lg_ref, lg2_ref, p_ref, p2_ref, m_ref, mn_ref, acc_ref,
                         *, tk, n_sel):
    tq = qi_ref.shape[0]
    nsb = tk // KEY_BLOCK
    i = pl.program_id(1)
    nkb = ((i + 1) * tq + tk - 1) // tk
    lane = lax.broadcasted_iota(I32, (1, LANES), 1)
    half_mask = [jnp.where(lane < 64, 1.0, 0.0).astype(BF16), jnp.where(lane >= 64, 1.0, 0.0).astype(BF16)]

    def stack_heads(src_ref, dst, first):
        for r in range(A_REP):
            h = first + r
            dst[r * tq:(r + 1) * tq, :] = src_ref[:, (h // 2) * LANES:(h // 2 + 1) * LANES] * half_mask[h % 2]

    wi = wi_ref[...]
    stack_heads(qi_ref, qm_ref.at[0], 0)
    q_chunk = (i * tq + lax.broadcasted_iota(I32, (tq, 1), 0)) >> 6

    def score_block(kb, _):
        for sb in range(nsb):
            dots = _dot(qm_ref[0], kidT_ref[kb * nsb + sb])
            s = jnp.zeros((tq, KEY_BLOCK), F32)
            for h in range(IDX_HEADS):
                s = s + jnp.maximum(dots[h * tq:(h + 1) * tq, :], 0.0) * wi[:, h:h + 1]
            k_chunk = (kb * tk + sb * KEY_BLOCK + lax.broadcasted_iota(I32, (1, KEY_BLOCK), 1)) >> 6
            s = jnp.where(k_chunk <= q_chunk, jnp.where(jnp.abs(s) < MIN_NORMAL, 0.0, s), -jnp.inf)
            sc_ref[kb, :, sb * KEY_BLOCK:(sb + 1) * KEY_BLOCK] = s
        return 0
    lax.fori_loop(0, nkb, score_block, 0)

    thr = _select_top(sc_ref, nkb, tq, tk, n_sel)

    m_ref[...] = jnp.full(m_ref.shape, NEG_BIG, F32)
    acc_ref[...] = jnp.zeros(acc_ref.shape, F32)
    for g in range(A_KV_HEADS):
        stack_heads(qa_ref, qm_ref.at[g], g * A_REP)

    def masked_logits(kb, lg, bias):
        bias[...] = jnp.where(sc_ref[kb] >= thr, 0.0, MASKED)
        for g in range(A_KV_HEADS):
            for sb in range(nsb):
                cols = slice(sb * KEY_BLOCK, (sb + 1) * KEY_BLOCK)
                logits = _dot(qm_ref[g], kadT_ref[kb * nsb + sb, g * LANES:(g + 1) * LANES, :])
                for r in range(A_REP):
                    lg[g, r * tq:(r + 1) * tq, cols] = logits[r * tq:(r + 1) * tq, :] + bias[:, cols]

    def accumulate(kb, lg, p):
        start = pl.multiple_of(kb * tk, tk)
        for g in range(A_KV_HEADS):
            row_max = jnp.max(_lane_fold(lg[g], jnp.maximum), axis=1, keepdims=True)
            mn_ref[g] = jnp.maximum(m_ref[g], row_max)
            p[g] = jnp.exp2(lg[g] - mn_ref[g]).astype(BF16)
            vblk = vad_ref[pl.ds(start, tk), g * LANES:(g + 1) * LANES]
            acc_ref[g] = jnp.exp2(m_ref[g] - mn_ref[g]) * acc_ref[g] + _dot(p[g], vblk)
            m_ref[g] = mn_ref[g]

    masked_logits(0, lg_ref, bias_ref)

    def attn_pair(j, _):
        kb = 2 * j
        masked_logits(kb + 1, lg2_ref, bias2_ref)
        accumulate(kb, lg_ref, p_ref)
        masked_logits(jnp.minimum(kb + 2, nkb - 1), lg_ref, bias_ref)
        accumulate(kb + 1, lg2_ref, p2_ref)
        return 0
    lax.fori_loop(0, nkb // 2, attn_pair, 0)

    @pl.when(nkb % 2 == 1)
    def _():
        accumulate(nkb - 1, lg_ref, p_ref)

    low64 = lax.broadcasted_iota(I32, (tq, LANES), 1) < 64
    for j in range(A_HEADS // 2):
        g, r = (2 * j) // A_REP, (2 * j) % A_REP
        a0 = acc_ref[g, r * tq:(r + 1) * tq, :]
        a1 = acc_ref[g, (r + 1) * tq:(r + 2) * tq, :]
        lo = a0 / a0[:, 64:65]
        hi = pltpu.roll(a1, 64, 1) / a1[:, 64:65]
        o_ref[:, j * LANES:(j + 1) * LANES] = jnp.where(low64, lo, hi).astype(BF16)


def _mix_a_prompt(p, b, t, tq, tk):
    nq = t // tq
    n_sel = min(TOPK_MAX, t // 4)
    nkb_max = t // tk
    nsub = t // KEY_BLOCK
    qspec = lambda w: pl.BlockSpec((tq, w), lambda bb, i: (bb * nq + i, 0))
    ktspec = lambda w: pl.BlockSpec((None, nsub, w, KEY_BLOCK), lambda bb, i: (bb, 0, 0, 0))
    return pl.pallas_call(
        functools.partial(_mix_a_prompt_kernel, tk=tk, n_sel=n_sel),
        grid=(b, nq),
        in_specs=[qspec(256), qspec(WI_PAD), ktspec(128), qspec(512), ktspec(256),
                  pl.BlockSpec((t, 256), lambda bb, i: (bb, 0))],
        out_specs=qspec(512),
        out_shape=jax.ShapeDtypeStruct((b * t, 512), BF16),
        scratch_shapes=[
            pltpu.VMEM((nkb_max, tq, tk), F32),
            pltpu.VMEM((tq, tk), F32),
            pltpu.VMEM((tq, tk), F32),
            pltpu.VMEM((A_KV_HEADS, A_REP * tq, LANES), BF16),
            pltpu.VMEM((A_KV_HEADS, A_REP * tq, tk), F32),
            pltpu.VMEM((A_KV_HEADS, A_REP * tq, tk), F32),
            pltpu.VMEM((A_KV_HEADS, A_REP * tq, tk), BF16),
            pltpu.VMEM((A_KV_HEADS, A_REP * tq, tk), BF16),
            pltpu.VMEM((A_KV_HEADS, A_REP * tq, 1), F32),
            pltpu.VMEM((A_KV_HEADS, A_REP * tq, 1), F32),
            pltpu.VMEM((A_KV_HEADS, A_REP * tq, LANES), F32),
        ],
        compiler_params=_params("arbitrary", "arbitrary"),
        name="mix_a_prompt",
    )(p["qi"], p["wi"], p["kidT"], p["qa"], p["kadT"], p["vad"])


def _softplus(z):
    return jnp.maximum(z, 0.0) + jnp.log(1.0 + jnp.exp(-jnp.abs(z)))


def _mix_b_prompt_kernel(q_ref, k_ref, v_ref, tri_ref, o_ref, r_ref, acc_ref):
    tq = q_ref.shape[0]
    i = pl.program_id(2)
    lane = lax.broadcasted_iota(I32, (1, LANES), 1)
    half_mask = [jnp.where(lane < 64, 1.0, 0.0).astype(BF16), jnp.where(lane >= 64, 1.0, 0.0).astype(BF16)]
    q = q_ref[...]
    qs = jnp.concatenate([q * half_mask[0], q * half_mask[1]], axis=0)
    tri = tri_ref[...]
    r_ref[...] = jnp.zeros(r_ref.shape, F32)
    acc_ref[...] = jnp.zeros(acc_ref.shape, F32)

    def block(kb, diag):
        start = pl.multiple_of(kb * tq, tq)
        z = _dot(qs, k_ref[kb])
        sp = _softplus(z)
        log_keep = -sp
        if diag:
            local_q = lax.broadcasted_iota(I32, (2 * tq, tq), 0) & (tq - 1)
            earlier = lax.broadcasted_iota(I32, (2 * tq, tq), 1) < local_q
            log_keep = jnp.where(earlier, log_keep, 0.0)
        lh, ll = _split(log_keep)
        suffix = _dot(jnp.concatenate([lh, ll], axis=0), tri)
        between = suffix[0:2 * tq, :] + suffix[2 * tq:4 * tq, :] + r_ref[...]
        att = jnp.exp(z - sp + between)
        if diag:
            att = jnp.where(earlier, att, 0.0)
        acc_ref[...] += _dot(att.astype(BF16), v_ref[pl.ds(start, tq), :])
        r_ref[...] += jnp.sum(log_keep, axis=1, keepdims=True)

    def live():
        return jnp.where(jnp.max(r_ref[...]) >= STICK_CUTOFF, 1, 0)

    block(i, True)

    def body(c):
        block(i - 1 - c[0], False)
        return c[0] + 1, live()
    lax.while_loop(lambda c: (c[0] < i) & (c[1] > 0), body, (jnp.int32(0), live()))

    low64 = lax.broadcasted_iota(I32, (tq, LANES), 1) < 64
    o_ref[...] = jnp.where(low64, acc_ref[0:tq, :], acc_ref[tq:2 * tq, :]).astype(BF16)


def _strict_lower(n):
    j = np.arange(n)[:, None]
    s = np.arange(n)[None, :]
    return jnp.asarray((j > s).astype(np.float32), dtype=BF16)


def _mix_b_prompt(p, b, t, tq):
    assert tq == KEY_BLOCK
    nq = t // tq
    npair = B_HEADS // 2
    return pl.pallas_call(
        _mix_b_prompt_kernel,
        grid=(b, npair, nq),
        in_specs=[
            pl.BlockSpec((tq, LANES), lambda bb, j, i: (bb * nq + i, j)),
            pl.BlockSpec((None, nq, LANES, KEY_BLOCK), lambda bb, j, i: (bb, 0, j, 0)),
            pl.BlockSpec((t, LANES), lambda bb, j, i: (bb, j)),
            pl.BlockSpec((tq, tq), lambda bb, j, i: (0, 0)),
        ],
        out_specs=pl.BlockSpec((tq, LANES), lambda bb, j, i: (bb * nq + i, j)),
        out_shape=jax.ShapeDtypeStruct((b * t, B_HEADS * HEAD_DIM), BF16),
        scratch_shapes=[pltpu.VMEM((2 * tq, 1), F32), pltpu.VMEM((2 * tq, LANES), F32)],
        compiler_params=_params("arbitrary", "arbitrary", "arbitrary"),
        name="mix_b_prompt",
    )(p["qb"], p["kbT"], p["vbh"], _strict_lower(tq))


def _mix_a_sample_kernel(qi_ref, wi_ref, cki_ref, nki_ref, qa_ref, cka_ref, cva_ref, nka_ref, nva_ref, o_ref,
                         sc_ref, pad_ref, m_ref, l_ref, acc_ref, *, tk, n_sel):
    t = wi_ref.shape[0]
    past = cki_ref.shape[1]
    nkb = past // tk

    wi = wi_ref[...]
    qi = qi_ref[...]
    col = lax.broadcasted_iota(I32, (t, tk), 1)

    def scores(kT):
        d = _dot(qi, kT)
        s = jnp.zeros((t, tk), F32)
        for h in range(IDX_HEADS):
            s = s + jnp.maximum(d[h * t:(h + 1) * t, :], 0.0) * wi[:, h:h + 1]
        return jnp.where(jnp.abs(s) < MIN_NORMAL, 0.0, s)

    for kb in range(nkb):
        sc_ref[kb] = scores(cki_ref[:, kb * tk:(kb + 1) * tk].astype(BF16))
    pad_ref[...] = jnp.zeros(pad_ref.shape, F32)
    pad_ref[0, :, 0:t] = nki_ref[...]
    s_new = scores(pad_ref[0].astype(BF16))
    q_chunk = (past + lax.broadcasted_iota(I32, (t, 1), 0)) >> 6
    admissible = (col < t) & (((past + col) >> 6) <= q_chunk)
    sc_ref[nkb] = jnp.where(admissible, s_new, -jnp.inf)

    thr = _select_top(sc_ref, nkb + 1, t, tk, n_sel)

    m_ref[...] = jnp.full(m_ref.shape, NEG_BIG, F32)
    l_ref[...] = jnp.zeros(l_ref.shape, F32)
    acc_ref[...] = jnp.zeros(acc_ref.shape, F32)

    thr_rep = jnp.concatenate([thr] * A_REP, axis=0)

    def attend(kb, k_of, v_of):
        sel = jnp.concatenate([sc_ref[kb]] * A_REP, axis=0) >= thr_rep
        for g in range(A_KV_HEADS):
            logits = jnp.where(sel, _dot(qa_ref[g], k_of(g)), NEG_BIG)
            m_old = m_ref[g]
            m_new = jnp.maximum(m_old, jnp.max(logits, axis=1, keepdims=True))
            alpha = jnp.exp2(m_old - m_new)
            p = jnp.where(sel, jnp.exp2(logits - m_new), 0.0)
            l_ref[g] = alpha * l_ref[g] + jnp.sum(p, axis=1, keepdims=True)
            acc_ref[g] = alpha * acc_ref[g] + _dot_nt(p.astype(BF16), v_of(g))
            m_ref[g] = m_new

    for kb in range(nkb):
        cols = slice(kb * tk, (kb + 1) * tk)
        attend(kb, lambda g: cka_ref[g, :, cols].astype(BF16), lambda g: cva_ref[g, :, cols].astype(BF16))
    pad_ref[:, :, 0:t] = nka_ref[...]
    knew = pad_ref[...].astype(BF16)
    pad_ref[:, :, 0:t] = nva_ref[...]
    vnew = pad_ref[...].astype(BF16)
    attend(nkb, lambda g: knew[g], lambda g: vnew[g])

    for g in range(A_KV_HEADS):
        o_ref[g] = acc_ref[g] / l_ref[g]


def _mix_a_sample(qi_hm, wi, cache_idx_kT, nikT, qa_hm, cache_a_kT, cache_a_vT, nakT, navT, layer, tk):
    b, t, _ = wi.shape
    past = cache_idx_kT.shape[3]
    assert past % tk == 0
    n_sel = min(TOPK_MAX, (past + t) // 4)
    rows = A_REP * t
    per_b = lambda *blk: pl.BlockSpec((None,) + blk, lambda bb: (bb,) + (0,) * len(blk))
    kv_cache = pl.BlockSpec((None, None, A_KV_HEADS, HEAD_DIM, past), lambda bb: (layer, bb, 0, 0, 0))
    return pl.pallas_call(
        functools.partial(_mix_a_sample_kernel, tk=tk, n_sel=n_sel),
        grid=(b,),
        in_specs=[per_b(IDX_HEADS * t, IDX_DIM), per_b(t, WI_PAD),
                  pl.BlockSpec((None, None, IDX_DIM, past), lambda bb: (layer, bb, 0, 0)), per_b(IDX_DIM, t),
                  per_b(A_KV_HEADS, rows, HEAD_DIM), kv_cache, kv_cache,
                  per_b(A_KV_HEADS, HEAD_DIM, t), per_b(A_KV_HEADS, HEAD_DIM, t)],
        out_specs=per_b(A_KV_HEADS, rows, HEAD_DIM),
        out_shape=jax.ShapeDtypeStruct((b, A_KV_HEADS, rows, HEAD_DIM), F32),
        scratch_shapes=[
            pltpu.VMEM((past // tk + 1, t, tk), F32),
            pltpu.VMEM((A_KV_HEADS, HEAD_DIM, tk), F32),
            pltpu.VMEM((A_KV_HEADS, rows, 1), F32),
            pltpu.VMEM((A_KV_HEADS, rows, 1), F32),
            pltpu.VMEM((A_KV_HEADS, rows, HEAD_DIM), F32),
        ],
        compiler_params=_params("arbitrary"),
        name="mix_a_sample",
    )(qi_hm, wi, cache_idx_kT, nikT, qa_hm, cache_a_kT, cache_a_vT, nakT, navT)


def _mix_b_sample_kernel(q_ref, nk_ref, nv_ref, tri_ref, ck_hbm, cv_hbm, o_ref,
                         kbuf, vbuf, sem, r_ref, acc_ref, pad_ref, *, layer, tkc):
    b = pl.program_id(0)
    heads, t, _ = q_ref.shape
    nblk = ck_hbm.shape[4] // tkc
    npad = pad_ref.shape[3]

    def copies(j, slot):
        cols = pl.ds(pl.multiple_of(j * tkc, tkc), tkc)
        return (pltpu.make_async_copy(ck_hbm.at[layer, b, :, :, cols], kbuf.at[slot], sem.at[0, slot]),
                pltpu.make_async_copy(cv_hbm.at[layer, b, :, :, cols], vbuf.at[slot], sem.at[1, slot]))

    def block(k_of, v_of, nk, earlier):
        z = jnp.concatenate([_dot(q_ref[h], k_of(h)) for h in range(heads)], axis=0)
        sp = _softplus(z)
        log_keep = -sp
        if earlier is not None:
            log_keep = jnp.where(earlier, log_keep, 0.0)
        lh, ll = _split(log_keep)
        suffix = _dot(jnp.concatenate([lh, ll], axis=0), tri_ref[0:nk, 0:nk])
        rows = heads * t
        att = jnp.exp(z - sp + suffix[0:rows, :] + suffix[rows:2 * rows, :] + r_ref[...])
        if earlier is not None:
            att = jnp.where(earlier, att, 0.0)
        att = att.astype(BF16)
        for h in range(heads):
            acc_ref[h] += _dot_nt(att[h * t:(h + 1) * t, :], v_of(h))
        r_ref[...] += jnp.sum(log_keep, axis=1, keepdims=True)

    def live():
        return jnp.where(jnp.max(r_ref[...]) >= STICK_CUTOFF, 1, 0)

    for c in copies(nblk - 1, 0):
        c.start()

    r_ref[...] = jnp.zeros(r_ref.shape, F32)
    acc_ref[...] = jnp.zeros(acc_ref.shape, F32)
    pad_ref[...] = jnp.zeros(pad_ref.shape, F32)
    pad_ref[0, :, :, 0:t] = nk_ref[...]
    pad_ref[1, :, :, 0:t] = nv_ref[...]
    key_j = lax.broadcasted_iota(I32, (heads * t, npad), 1)
    query_i = lax.broadcasted_iota(I32, (heads * t, npad), 0) & (t - 1)
    block(lambda h: pad_ref[0, h].astype(BF16), lambda h: pad_ref[1, h].astype(BF16), npad, key_j < query_i)

    for c in copies(nblk - 1, 0):
        c.wait()

    def body(c):
        n = c[0]
        slot = n & 1
        block(lambda h: kbuf[slot, h].astype(BF16), lambda h: vbuf[slot, h].astype(BF16), tkc, None)
        alive = live()

        @pl.when((n + 1 < nblk) & (alive > 0))
        def _():
            nxt = copies(nblk - 2 - n, 1 - slot)
            for cp in nxt:
                cp.start()
            for cp in nxt:
                cp.wait()
        return n + 1, alive
    lax.while_loop(lambda c: (c[0] < nblk) & (c[1] > 0), body, (jnp.int32(0), live()))

    o_ref[...] = jnp.concatenate([acc_ref[h] for h in range(heads)], axis=1)


def _mix_b_sample(q, cache_kT, cache_vT, nkT, nvT, layer, tkc):
    b, heads, t, d = q.shape
    assert t & (t - 1) == 0 and heads * t == LANES and tkc >= LANES
    assert cache_kT.shape[4] % tkc == 0
    per_b = lambda *blk: pl.BlockSpec((None,) + blk, lambda bb: (bb,) + (0,) * len(blk))
    tri = _strict_lower(tkc)
    buf = (2, heads, d, tkc)
    return pl.pallas_call(
        functools.partial(_mix_b_sample_kernel, layer=layer, tkc=tkc),
        grid=(b,),
        in_specs=[per_b(heads, t, d), per_b(heads, d, t), per_b(heads, d, t), pl.BlockSpec((tkc, tkc), lambda bb: (0, 0)),
                  pl.BlockSpec(memory_space=pl.ANY), pl.BlockSpec(memory_space=pl.ANY)],
        out_specs=per_b(t, heads * d),
        out_shape=jax.ShapeDtypeStruct((b, t, heads * d), F32),
        scratch_shapes=[pltpu.VMEM(buf, F32), pltpu.VMEM(buf, F32), pltpu.SemaphoreType.DMA((2, 2)),
                        pltpu.VMEM((heads * t, 1), F32), pltpu.VMEM((heads, t, d), F32),
                        pltpu.VMEM((2, heads, d, LANES), F32)],
        compiler_params=_params("arbitrary"),
        name="mix_b_sample",
    )(q, nkT, nvT, tri, cache_kT, cache_vT)


def _merge_kernel(x_ref, oa_ref, ob_ref, gate_ref, g1_ref, sh2_ref, sc2_ref, n2_ref,
                  wa_ref, wb_ref, wo_ref, wr_ref, br_ref, x1_o, h2_o, rg_o):
    nb, tr, d = x_ref.shape
    tm = nb * tr
    a = _dot(oa_ref[...], wa_ref[...])
    bb = _dot(ob_ref[...], wb_ref[...])
    merged = gate_ref[:, 0:d] * a + gate_ref[:, d:2 * d] * bb
    mix = _dot(merged.astype(BF16), wo_ref[...])
    x1 = x_ref[...] + g1_ref[...] * mix.reshape(nb, tr, d)
    x1_o[...] = x1
    ms = jnp.mean(x1 * x1, axis=-1, keepdims=True)
    h2 = ((x1 * lax.rsqrt(ms + RMS_EPS) * n2_ref[...]) * (1.0 + sc2_ref[...]) + sh2_ref[...]).reshape(tm, d)
    h2_o[...] = h2.astype(BF16)

    logits = _dot3(h2, wr_ref[...])
    e = jnp.exp(logits - jnp.max(logits, axis=1, keepdims=True))
    aff = e / jnp.sum(e, axis=1, keepdims=True)
    biased = aff + br_ref[...]
    col = [biased[:, j:j + 1] for j in range(N_EXPERTS)]
    best_val = None
    for g in range(N_GROUPS):
        c = col[g * EXP_PER_GROUP:(g + 1) * EXP_PER_GROUP]
        score = None
        for u in range(EXP_PER_GROUP):
            for w in range(u + 1, EXP_PER_GROUP):
                pair = c[u] + c[w]
                score = pair if score is None else jnp.maximum(score, pair)
        if best_val is None:
            best_val, best_g = score, jnp.zeros((tm, 1), I32)
        else:
            upd = score > best_val
            best_g = jnp.where(upd, g, best_g)
            best_val = jnp.where(upd, score, best_val)
    eidx = lax.broadcasted_iota(I32, (tm, N_EXPERTS), 1)
    cand = jnp.where((eidx >> 2) == best_g, biased, -jnp.inf)
    m1 = jnp.max(cand, axis=1, keepdims=True)
    i1 = jnp.min(jnp.where(cand == m1, eidx, N_EXPERTS), axis=1, keepdims=True)
    cand2 = jnp.where(eidx == i1, -jnp.inf, cand)
    m2 = jnp.max(cand2, axis=1, keepdims=True)
    i2 = jnp.min(jnp.where(cand2 == m2, eidx, N_EXPERTS), axis=1, keepdims=True)
    w1 = jnp.sum(jnp.where(eidx == i1, aff, 0.0), axis=1, keepdims=True)
    w2 = jnp.sum(jnp.where(eidx == i2, aff, 0.0), axis=1, keepdims=True)
    tot = w1 + w2
    rg_o[...] = jnp.where(eidx == i1, w1 / tot, 0.0) + jnp.where(eidx == i2, w2 / tot, 0.0)


def _merge(x, oa, ob, gates, mod, layer, boff, n2, wa, wb, wo, wr, br, nb, tr):
    b, t, d = x.shape
    tpb = t // tr
    tm = nb * tr
    rows = b * t
    grid = (b // nb) * tpb

    def modspec(which):
        return pl.BlockSpec((None, None, nb, 1, d), lambda i: (layer, which, boff // nb + i // tpb, 0, 0))

    xspec = pl.BlockSpec((nb, tr, d), lambda i: (i // tpb, i % tpb, 0))
    row = lambda w: pl.BlockSpec((tm, w), lambda i: (i, 0))
    lw = lambda r, c: pl.BlockSpec((None, r, c), lambda i: (layer, 0, 0))
    return pl.pallas_call(
        _merge_kernel, grid=(grid,),
        in_specs=[xspec, row(512), row(512), row(2 * d), modspec(2), modspec(3), modspec(4), lw(1, d),
                  lw(512, d), lw(512, d), lw(d, d),
                  pl.BlockSpec((d, N_EXPERTS), lambda i: (0, 0)), pl.BlockSpec((1, N_EXPERTS), lambda i: (0, 0))],
        out_specs=[xspec, row(d), row(N_EXPERTS)],
        out_shape=[jax.ShapeDtypeStruct((b, t, d), F32), jax.ShapeDtypeStruct((rows, d), BF16),
                   jax.ShapeDtypeStruct((rows, N_EXPERTS), F32)],
        compiler_params=_params("arbitrary"), name="merge",
    )(x, oa, ob, gates, mod, mod, mod, n2, wa, wb, wo, wr, br)


def _moe_kernel(x_ref, h_ref, rg_ref, g2_ref, wg_ref, wu_ref, wd_ref, o_ref, acc_ref):
    nb, tr, d = x_ref.shape
    tm = nb * tr
    e = pl.program_id(1)

    @pl.when(e == 0)
    def _():
        acc_ref[...] = jnp.zeros(acc_ref.shape, F32)

    h = h_ref[...]
    gate = _dot(h, wg_ref[...])
    up = _dot(h, wu_ref[...])
    hidden = (gate * jax.nn.sigmoid(gate)) * up
    y = _dot(hidden.astype(BF16), wd_ref[...])
    eidx = lax.broadcasted_iota(I32, (tm, N_EXPERTS), 1)
    w = jnp.sum(jnp.where(eidx == e, rg_ref[...], 0.0), axis=1, keepdims=True)
    acc_ref[...] += w * y

    @pl.when(e == N_EXPERTS - 1)
    def _():
        o_ref[...] = x_ref[...] + g2_ref[...] * acc_ref[...].reshape(nb, tr, d)


def _moe(x1, h2, rg, mod, layer, boff, wg, wu, wd, nb, tr):
    b, t, d = x1.shape
    tpb = t // tr
    tm = nb * tr
    grid = (b // nb) * tpb
    de = wg.shape[-1]
    xspec = pl.BlockSpec((nb, tr, d), lambda i, e: (i // tpb, i % tpb, 0))
    return pl.pallas_call(
        _moe_kernel, grid=(grid, N_EXPERTS),
        in_specs=[xspec, pl.BlockSpec((tm, d), lambda i, e: (i, 0)), pl.BlockSpec((tm, N_EXPERTS), lambda i, e: (i, 0)),
                  pl.BlockSpec((None, None, nb, 1, d), lambda i, e: (layer, 5, boff // nb + i // tpb, 0, 0)),
                  pl.BlockSpec((None, None, d, de), lambda i, e: (layer, e, 0, 0)),
                  pl.BlockSpec((None, None, d, de), lambda i, e: (layer, e, 0, 0)),
                  pl.BlockSpec((None, None, de, d), lambda i, e: (layer, e, 0, 0))],
        out_specs=xspec,
        out_shape=jax.ShapeDtypeStruct((b, t, d), F32),
        scratch_shapes=[pltpu.VMEM((tm, d), F32)],
        compiler_params=_params("arbitrary", "arbitrary"), name="moe",
    )(x1, h2, rg, mod, wg, wu, wd)


def _rope_tables(pos):
    inv_freq = ROPE_THETA ** (-jnp.arange(0, HEAD_DIM, 2, dtype=F32) / HEAD_DIM)
    ang = pos.astype(F32)[:, None] * inv_freq[None, :]
    c, s = jnp.cos(ang), jnp.sin(ang)
    cos = jnp.concatenate([c, c, c, c], axis=1)
    sin = jnp.concatenate([-s, s, -s, s], axis=1)
    return cos, sin


def _widen_w_in(w_in):
    sizes = (512, 128, 128, 256, 64, 4, 512, 512, 512, 2048)
    pts = [int(v) for v in np.cumsum(sizes)[:-1]]
    qa, ka, va, qi, ki, wi, qb, kb, vb, gl = jnp.split(w_in, pts, axis=-1)
    dup = lambda w: jnp.concatenate([w[..., 0:64], w[..., 0:64], w[..., 64:128], w[..., 64:128]], axis=-1)
    wi_pad = jnp.pad(wi, ((0, 0), (0, 0), (0, LANES - wi.shape[-1])))
    ext = jnp.concatenate([qa, dup(ka), dup(va), qi, ki, ki, wi_pad, qb, kb, vb, gl], axis=-1)
    assert ext.shape[-1] == N_EXT
    return ext.astype(BF16)


def kernel(x_prompt, x_sample, cache_a_k, cache_a_v, cache_idx_k, cache_b_k, cache_b_v, c_prompt, c_sample,
           norm1_g, norm2_g, w_ada, b_ada, w_in, qn_g, kn_g, w_branch_a, w_branch_b, w_out, w_router, b_router,
           w_exp_gate, w_exp_up, w_exp_down):
    depth = w_in.shape[0]
    bp, tp, d = x_prompt.shape
    bs, ts, _ = x_sample.shape
    past = cache_a_k.shape[2]

    nc = -(-(bs + bp) // 8) * 8
    c_all = jnp.concatenate([c_sample, c_prompt, jnp.zeros((nc - bs - bp, d), F32)], axis=0)
    mod = _ada(c_all, w_ada, b_ada).reshape(depth, 6, nc, 1, d)
    boff_s, boff_p = 0, bs

    w_ext = _widen_w_in(w_in)
    wa, wb, wo = w_branch_a.astype(BF16), w_branch_b.astype(BF16), w_out.astype(BF16)
    wg, wu, wd = w_exp_gate.astype(BF16), w_exp_up.astype(BF16), w_exp_down.astype(BF16)
    n1 = norm1_g.reshape(depth, 1, d)
    n2 = norm2_g.reshape(depth, 1, d)
    qn = jnp.tile(qn_g, (1, 2)).reshape(depth, 1, LANES)
    kn = jnp.tile(kn_g, (1, 2)).reshape(depth, 1, LANES)
    seg = jnp.asarray(np.kron(np.eye(2), np.ones((64, 64))).astype(np.float32), dtype=BF16)
    br = b_router.reshape(1, N_EXPERTS)

    cos_p, sin_p = _rope_tables(jnp.arange(tp, dtype=I32))
    cos_s, sin_s = _rope_tables(past + jnp.arange(ts, dtype=I32))
    cos_s, sin_s = jnp.tile(cos_s, (bs, 1)), jnp.tile(sin_s, (bs, 1))

    tr_p = min(256, tp)
    tq_a = min(256, tp)
    tk_a = min(512, tp)
    tq_b = min(256, tp)
    tm_moe = min(1024, tp)
    tk_s = min(512, past)

    cache_b_kT = jnp.transpose(cache_b_k, (0, 1, 3, 4, 2))
    cache_b_vT = jnp.transpose(cache_b_v, (0, 1, 3, 4, 2))
    cache_a_kT = jnp.transpose(cache_a_k, (0, 1, 3, 4, 2))
    cache_a_vT = jnp.transpose(cache_a_v, (0, 1, 3, 4, 2))
    cache_idx_kT = jnp.transpose(cache_idx_k, (0, 1, 3, 2))

    xp, xs = x_prompt, x_sample
    rows_p, rows_s = [], []
    for l in range(depth):
        p = _in_proj(xp, mod, l, boff_p, n1, w_ext, cos_p, sin_p, qn, kn, seg, 1, tr_p, True)
        oa = _mix_a_prompt(p, bp, tp, tq_a, tk_a)
        ob = _mix_b_prompt(p, bp, tp, tq_b)
        x1, h2, rg = _merge(xp, oa, ob, p["gates"], mod, l, boff_p, n2, wa, wb, wo, w_router, br, 1, tr_p)
        xp = _moe(x1, h2, rg, mod, l, boff_p, wg, wu, wd, 1, tm_moe)
        rows_p.append(p)

        s = _in_proj(xs, mod, l, boff_s, n1, w_ext, cos_s, sin_s, qn, kn, seg, bs, ts, False)
        qi_hm = s["qi"].reshape(bs, ts, IDX_HEADS, IDX_DIM).transpose(0, 2, 1, 3).reshape(bs, IDX_HEADS * ts, IDX_DIM)
        qa5 = s["qa"].reshape(bs, ts, A_KV_HEADS, A_REP, HEAD_DIM).transpose(0, 2, 3, 1, 4)
        qa5 = qa5.reshape(bs, A_KV_HEADS, A_REP * ts, HEAD_DIM)
        kv_major = lambda a: a.reshape(bs, ts, A_KV_HEADS, HEAD_DIM).transpose(0, 2, 3, 1)
        oa_s = _mix_a_sample(qi_hm, s["wi"].reshape(bs, ts, WI_PAD), cache_idx_kT,
                             s["nik"].reshape(bs, ts, IDX_DIM).transpose(0, 2, 1), qa5, cache_a_kT, cache_a_vT,
                             kv_major(s["nak"]), kv_major(s["nav"]), l, tk_s)
        oa_s = oa_s.reshape(bs, A_KV_HEADS, A_REP, ts, HEAD_DIM).transpose(0, 3, 1, 2, 4).reshape(bs * ts, 512)
        heads_first = lambda a: a.reshape(bs, ts, B_HEADS, HEAD_DIM).transpose(0, 2, 1, 3)
        feature_major = lambda a: a.reshape(bs, ts, B_HEADS, HEAD_DIM).transpose(0, 2, 3, 1)
        ob_s = _mix_b_sample(heads_first(s["qb"]), cache_b_kT, cache_b_vT, feature_major(s["nbk"]),
                             feature_major(s["nbv"]), l, min(256, past))
        x1s, h2s, rgs = _merge(xs, oa_s.astype(BF16), ob_s.reshape(bs * ts, 512).astype(BF16), s["gates"], mod, l,
                               boff_s, n2, wa, wb, wo, w_router, br, bs, ts)
        xs = _moe(x1s, h2s, rgs, mod, l, boff_s, wg, wu, wd, bs, ts)
        rows_s.append(s)

    def stack(rows, name, b, t, tail):
        return jnp.stack([r[name].reshape((b, t) + tail) for r in rows], axis=0)

    def stack_t(rows, name, tail):
        a = jnp.stack([r[name] for r in rows], axis=0).reshape((depth, bp) + tail + (tp,))
        return jnp.moveaxis(a, -1, 2)

    kv = (A_KV_HEADS, HEAD_DIM)
    bh = (B_HEADS, HEAD_DIM)
    return (xp, xs,
            stack_t(rows_p, "nak", kv), stack_t(rows_p, "nav", kv), stack_t(rows_p, "nik", (IDX_DIM,)),
            stack_t(rows_p, "nbk", bh), stack_t(rows_p, "nbv", bh),
            stack(rows_s, "nak", bs, ts, kv), stack(rows_s, "nav", bs, ts, kv), stack(rows_s, "nik", bs, ts, (IDX_DIM,)),
            stack(rows_s, "nbk", bs, ts, bh), stack(rows_s, "nbv", bs, ts, bh))
```

```python
import functools

import jax
import jax.numpy as jnp
import numpy as np
from jax import lax
from jax.experimental import pallas as pl
from jax.experimental.pallas import tpu as pltpu

F32 = jnp.float32
BF16 = jnp.bfloat16
I32 = jnp.int32

HEAD_DIM = 64
A_HEADS = 8
A_KV_HEADS = 2
A_REP = A_HEADS // A_KV_HEADS
IDX_HEADS = 4
IDX_DIM = 64
B_HEADS = 8
CHUNK = 64
TOPK_MAX = 256
ROPE_THETA = 10000.0
N_EXPERTS = 16
N_GROUPS = 4
EXP_PER_GROUP = N_EXPERTS // N_GROUPS
RMS_EPS = 1e-6
LANES = 128
LOG2_E = 1.4426950408889634
A_Q_SCALE = HEAD_DIM ** -0.5 * LOG2_E
NEG_BIG = -1e30
MASKED = -3e30
STICK_CUTOFF = -120.0
INT_MIN = -2147483648
KEY_NEG_INF = -2139095041
MIN_NORMAL_BITS = 0x00800000
MIN_NORMAL = 2.0 ** -126
FLOAT_LOWEST = -3.4028234663852886e38
SEARCH_STEPS_PER_TEST = 4

C_QA = 0
C_KAD = 512
C_VAD = 768
C_QI = 1024
C_KID = 1280
C_WI = 1408
C_QB = 1536
C_KB = 2048
C_VB = 2560
C_GL = 3072
N_EXT = 5120
WI_PAD = 8

VMEM_LIMIT = 56 * 1024 * 1024


def _dot(a, b):
    return jnp.dot(a, b, preferred_element_type=F32)


def _dot_nt(a, b):
    return lax.dot_general(a, b, (((1,), (1,)), ((), ())), preferred_element_type=F32)


def _split(a):
    hi = a.astype(BF16)
    lo = (a - hi.astype(F32)).astype(BF16)
    return hi, lo


def _dot3(a, b):
    ah, al = _split(a)
    bh, bl = _split(b)
    return _dot(ah, bh) + _dot(al, bh) + _dot(ah, bl)


def _params(*sem):
    return pltpu.CompilerParams(dimension_semantics=sem, vmem_limit_bytes=VMEM_LIMIT)


def _ada_kernel(c_ref, w_ref, b_ref, o_ref):
    c = c_ref[...]
    s = c * jax.nn.sigmoid(c)
    o_ref[...] = _dot3(s, w_ref[...]) + b_ref[...]


def _ada(c_all, w_ada, b_ada):
    depth, d, _ = w_ada.shape
    nc = c_all.shape[0]
    return pl.pallas_call(
        _ada_kernel,
        grid=(depth, 6),
        in_specs=[
            pl.BlockSpec((nc, d), lambda l, j: (0, 0)),
            pl.BlockSpec((None, d, d), lambda l, j: (l, 0, j)),
            pl.BlockSpec((None, 1, d), lambda l, j: (l, 0, j)),
        ],
        out_specs=pl.BlockSpec((None, None, nc, d), lambda l, j: (l, j, 0, 0)),
        out_shape=jax.ShapeDtypeStruct((depth, 6, nc, d), F32),
        compiler_params=_params("arbitrary", "arbitrary"),
        name="ada_mod",
    )(c_all, w_ada, b_ada.reshape(depth, 1, 6 * d))


def _rope(v, cos, sin, first_half):
    partner = jnp.where(first_half, pltpu.roll(v, LANES - 32, 1), pltpu.roll(v, 32, 1))
    return v * cos + partner * sin


_IN_PROJ_ROWS = (("qa", 512, BF16), ("qi", 256, BF16), ("wi", WI_PAD, F32), ("qb", 512, BF16), ("gates", 2048, F32))
_IN_PROJ_NEW_ROWS = (("nak", 128), ("nav", 128), ("nik", 64), ("nbk", 512), ("nbv", 512))
_IN_PROJ_PROMPT_ROWS = (("vad", 256, BF16), ("vbh", 512, BF16))
_IN_PROJ_PROMPT_KEYS = (("kadT", 256), ("kidT", 128), ("kbT", 512))
KEY_BLOCK = 256


def _in_proj_kernel(x_ref, sh_ref, sc_ref, g_ref, w_ref, cos_ref, sin_ref, qn_ref, kn_ref, seg_ref, *out_refs,
                    names):
    o = dict(zip(names, out_refs))
    prompt = "kadT" in o
    nb, tr, d = x_ref.shape
    tm = nb * tr
    x = x_ref[...]
    ms = jnp.mean(x * x, axis=-1, keepdims=True)
    h = (x * lax.rsqrt(ms + RMS_EPS) * g_ref[...]) * (1.0 + sc_ref[...]) + sh_ref[...]
    hb = h.reshape(tm, d).astype(BF16)

    def proj(lo, width):
        return _dot(hb, w_ref[:, lo:lo + width])

    cos = cos_ref[...]
    sin = sin_ref[...]
    seg = seg_ref[...]
    lane = lax.broadcasted_iota(I32, (tm, LANES), 1)
    first_half = (lane & 32) == 0
    low64 = lane < 64

    def head_norm(v, gain):
        v2 = v * v
        v2h, v2l = _split(v2)
        msq = (_dot(v2h, seg) + _dot(v2l, seg)) * (1.0 / HEAD_DIM)
        return v * lax.rsqrt(msq + RMS_EPS) * gain

    qn = qn_ref[...]
    kn = kn_ref[...]
    v = proj(C_QA, 512)
    for c in range(4):
        blk = _rope(head_norm(v[:, c * LANES:(c + 1) * LANES], qn), cos, sin, first_half)
        o["qa"][:, c * LANES:(c + 1) * LANES] = (blk * A_Q_SCALE).astype(BF16)
    v = proj(C_KAD, 256)
    kd = [_rope(head_norm(v[:, c * LANES:(c + 1) * LANES], kn), cos, sin, first_half) for c in range(2)]
    v = proj(C_VAD, 256)
    if prompt:
        for c in range(2):
            kT = kd[c].T
            o["kadT"][c * LANES:(c + 1) * LANES, :] = kT.astype(BF16)
            o["nak"][c * HEAD_DIM:(c + 1) * HEAD_DIM, :] = kT[0:HEAD_DIM, :]
            o["nav"][c * HEAD_DIM:(c + 1) * HEAD_DIM, :] = v[:, c * LANES:(c + 1) * LANES].T[0:HEAD_DIM, :]
        ones_col = jnp.where(lane == 64, 1.0, 0.0)
        for c in range(2):
            blk = jnp.where(low64, v[:, c * LANES:(c + 1) * LANES], ones_col)
            o["vad"][:, c * LANES:(c + 1) * LANES] = blk.astype(BF16)
    else:
        o["nak"][...] = jnp.where(low64, kd[0], kd[1])
        o["nav"][...] = jnp.where(low64, v[:, 0:LANES], v[:, LANES:2 * LANES])
    v = proj(C_QI, 256)
    for c in range(2):
        blk = _rope(v[:, c * LANES:(c + 1) * LANES], cos, sin, first_half)
        o["qi"][:, c * LANES:(c + 1) * LANES] = (blk * (IDX_DIM ** -0.5)).astype(BF16)
    v = _rope(proj(C_KID, LANES), cos, sin, first_half)
    if prompt:
        kT = v.T
        o["kidT"][...] = kT.astype(BF16)
        o["nik"][...] = kT[0:IDX_DIM, :]
    else:
        o["nik"][...] = v[:, 0:IDX_DIM]
    v = proj(C_WI, LANES)
    o["wi"][...] = v[:, 0:WI_PAD] * (IDX_HEADS ** -0.5)
    o["qb"][...] = (proj(C_QB, 512) * (HEAD_DIM ** -0.5)).astype(BF16)
    v = proj(C_KB, 512)
    if prompt:
        kT = v.T
        o["kbT"][...] = kT.astype(BF16)
        o["nbk"][...] = kT
    else:
        o["nbk"][...] = v
    v = proj(C_VB, 512)
    if prompt:
        o["vbh"][...] = v.astype(BF16)
        o["nbv"][...] = v.T
    else:
        o["nbv"][...] = v
    for c in range(4):
        o["gates"][:, c * 512:(c + 1) * 512] = jax.nn.sigmoid(proj(C_GL + c * 512, 512))


def _in_proj(x, mod, layer, boff, norm_g, w_ext, cos, sin, qn, kn, seg, nb, tr, prompt):
    b, t, d = x.shape
    tpb = t // tr
    tm = nb * tr
    rows = b * t
    grid = (b // nb) * tpb

    def modspec(which):
        return pl.BlockSpec((None, None, nb, 1, d), lambda i: (layer, which, boff // nb + i // tpb, 0, 0))

    in_specs = [
        pl.BlockSpec((nb, tr, d), lambda i: (i // tpb, i % tpb, 0)),
        modspec(0), modspec(1),
        pl.BlockSpec((None, 1, d), lambda i: (layer, 0, 0)),
        pl.BlockSpec((None, d, N_EXT), lambda i: (layer, 0, 0)),
        pl.BlockSpec((tm, LANES), lambda i: (i % tpb, 0)),
        pl.BlockSpec((tm, LANES), lambda i: (i % tpb, 0)),
        pl.BlockSpec((None, 1, LANES), lambda i: (layer, 0, 0)),
        pl.BlockSpec((None, 1, LANES), lambda i: (layer, 0, 0)),
        pl.BlockSpec((LANES, LANES), lambda i: (0, 0)),
    ]
    row_outs = _IN_PROJ_ROWS + (_IN_PROJ_PROMPT_ROWS if prompt else tuple((n, w, F32) for n, w in _IN_PROJ_NEW_ROWS))
    names = [n for n, _, _ in row_outs]
    out_specs = [pl.BlockSpec((tm, w), lambda i: (i, 0)) for _, w, _ in row_outs]
    out_shape = [jax.ShapeDtypeStruct((rows, w), dt) for _, w, dt in row_outs]
    if prompt:
        assert nb == 1 and tr == KEY_BLOCK
        for n, w in _IN_PROJ_PROMPT_KEYS:
            names.append(n)
            out_specs.append(pl.BlockSpec((None, None, w, KEY_BLOCK), lambda i: (i // tpb, i % tpb, 0, 0)))
            out_shape.append(jax.ShapeDtypeStruct((b, tpb, w, KEY_BLOCK), BF16))
        for n, w in _IN_PROJ_NEW_ROWS:
            names.append(n)
            out_specs.append(pl.BlockSpec((None, w, KEY_BLOCK), lambda i: (i // tpb, 0, i % tpb)))
            out_shape.append(jax.ShapeDtypeStruct((b, w, t), F32))
    outs = pl.pallas_call(
        functools.partial(_in_proj_kernel, names=tuple(names)),
        grid=(grid,), in_specs=in_specs, out_specs=out_specs, out_shape=out_shape,
        compiler_params=_params("arbitrary"), name="in_proj",
    )(x, mod, mod, norm_g, w_ext, cos, sin, qn, kn, seg)
    return dict(zip(names, outs))


def _candidate_float(key):
    bits = key ^ ((key >> 31) & 0x7FFFFFFF)
    bits = jnp.where((key > 0) & (key < MIN_NORMAL_BITS), MIN_NORMAL_BITS, bits)
    return lax.bitcast_convert_type(bits, F32)


def _lane_fold(m, op=jnp.add):
    acc = m[:, 0:LANES]
    for c in range(1, m.shape[1] // LANES):
        acc = op(acc, m[:, c * LANES:(c + 1) * LANES])
    return acc


def _select_top(sc_ref, nkb, rows, tk, n_sel):
    def count(compare, c):
        def body(kb, acc):
            return acc + _lane_fold(jnp.where(compare(sc_ref[kb], c), 1, 0))
        acc = lax.fori_loop(0, nkb, body, jnp.zeros((rows, LANES), I32))
        return jnp.sum(acc, axis=1, keepdims=True)

    def step(it, v, hit):
        cand = v ^ jnp.left_shift(jnp.int32(1), 31 - it)
        cnt = count(jnp.greater_equal, _candidate_float(cand))
        ok = cnt >= n_sel
        return jnp.where(ok, cand, v), hit | jnp.where(ok & (cnt == n_sel), 1, 0)

    def steps(c):
        g, v, hit, _ = c
        for u in range(SEARCH_STEPS_PER_TEST):
            v, hit = step(g * SEARCH_STEPS_PER_TEST + u, v, hit)
        return g + 1, v, hit, jnp.min(hit)
    init = (jnp.int32(0), jnp.full((rows, 1), INT_MIN, I32), jnp.zeros((rows, 1), I32), jnp.int32(0))
    _, v, hit, all_hit = lax.while_loop(lambda c: (c[0] < 32 // SEARCH_STEPS_PER_TEST) & (c[3] == 0), steps, init)
    thr = jnp.where(v < KEY_NEG_INF, -jnp.inf, _candidate_float(v))

    @pl.when(all_hit == 0)
    def _():
        cnt_ge = count(jnp.greater_equal, thr)
        need = n_sel - count(jnp.greater, thr)
        tied = (thr > -jnp.inf) & (cnt_ge > n_sel)

        @pl.when(jnp.max(jnp.where(tied, 1, 0)) > 0)
        def _():
            r_i = lax.broadcasted_iota(I32, (tk, tk), 0)
            c_i = lax.broadcasted_iota(I32, (tk, tk), 1)
            upper = jnp.where(r_i <= c_i, 1.0, 0.0).astype(BF16)
            need_f = need.astype(F32)

            def body(kb, carry):
                s = sc_ref[kb]
                eq = s == thr
                prefix = _dot(jnp.where(eq, 1.0, 0.0).astype(BF16), upper) + carry
                sc_ref[kb] = jnp.where(eq & (prefix > need_f), -jnp.inf, s)
                return prefix[:, tk - 1:tk]
            lax.fori_loop(0, nkb, body, jnp.zeros((rows, 1), F32))

    return jnp.maximum(thr, FLOAT_LOWEST)


def _mix_a_prompt_kernel(qi_ref, wi_ref, kidT_ref, qa_ref, kadT_ref, vad_ref, o_ref,
                         sc_ref, bias_ref, bias2_ref, qm_ref, lg_ref, lg2_ref, p_ref, p2_ref, m_ref, mn_ref, acc_ref,
                         *, tk, n_sel):
    tq = qi_ref.shape[0]
    nsb = tk // KEY_BLOCK
    i = pl.program_id(1)
    nkb = ((i + 1) * tq + tk - 1) // tk
    lane = lax.broadcasted_iota(I32, (1, LANES), 1)
    half_mask = [jnp.where(lane < 64, 1.0, 0.0).astype(BF16), jnp.where(lane >= 64, 1.0, 0.0).astype(BF16)]

    def stack_heads(src_ref, dst, first):
        for r in range(A_REP):
            h = first + r
            dst[r * tq:(r + 1) * tq, :] = src_ref[:, (h // 2) * LANES:(h // 2 + 1) * LANES] * half_mask[h % 2]

    wi = wi_ref[...]
    stack_heads(qi_ref, qm_ref.at[0], 0)
    q_chunk = (i * tq + lax.broadcasted_iota(I32, (tq, 1), 0)) >> 6

    def score_block(kb, _):
        for sb in range(nsb):
            dots = _dot(qm_ref[0], kidT_ref[kb * nsb + sb])
            s = jnp.zeros((tq, KEY_BLOCK), F32)
            for h in range(IDX_HEADS):
                s = s + jnp.maximum(dots[h * tq:(h + 1) * tq, :], 0.0) * wi[:, h:h + 1]
            k_chunk = (kb * tk + sb * KEY_BLOCK + lax.broadcasted_iota(I32, (1, KEY_BLOCK), 1)) >> 6
            s = jnp.where(k_chunk <= q_chunk, jnp.where(jnp.abs(s) < MIN_NORMAL, 0.0, s), -jnp.inf)
            sc_ref[kb, :, sb * KEY_BLOCK:(sb + 1) * KEY_BLOCK] = s
        return 0
    lax.fori_loop(0, nkb, score_block, 0)

    thr = _select_top(sc_ref, nkb, tq, tk, n_sel)

    m_ref[...] = jnp.full(m_ref.shape, NEG_BIG, F32)
    acc_ref[...] = jnp.zeros(acc_ref.shape, F32)
    for g in range(A_KV_HEADS):
        stack_heads(qa_ref, qm_ref.at[g], g * A_REP)

    def masked_logits(kb, lg, bias, g):
        if g == 0:
            bias[...] = jnp.where(sc_ref[kb] >= thr, 0.0, MASKED)
        for sb in range(nsb):
            cols = slice(sb * KEY_BLOCK, (sb + 1) * KEY_BLOCK)
            logits = _dot(qm_ref[g], kadT_ref[kb * nsb + sb, g * LANES:(g + 1) * LANES, :])
            for r in range(A_REP):
                lg[g, r * tq:(r + 1) * tq, cols] = logits[r * tq:(r + 1) * tq, :] + bias[:, cols]

    def accumulate(kb, lg, p, g):
        start = pl.multiple_of(kb * tk, tk)
        row_max = jnp.max(_lane_fold(lg[g], jnp.maximum), axis=1, keepdims=True)
        mn_ref[g] = jnp.maximum(m_ref[g], row_max)
        p[g] = jnp.exp2(lg[g] - mn_ref[g]).astype(BF16)
        vblk = vad_ref[pl.ds(start, tk), g * LANES:(g + 1) * LANES]
        acc_ref[g] = jnp.exp2(m_ref[g] - mn_ref[g]) * acc_ref[g] + _dot(p[g], vblk)
        m_ref[g] = mn_ref[g]

    for g in range(A_KV_HEADS):
        masked_logits(0, lg_ref, bias_ref, g)

    def attn_pair(j, _):
        kb = 2 * j
        for g in range(A_KV_HEADS):
            masked_logits(kb + 1, lg2_ref, bias2_ref, g)
            accumulate(kb, lg_ref, p_ref, g)
        for g in range(A_KV_HEADS):
            masked_logits(jnp.minimum(kb + 2, nkb - 1), lg_ref, bias_ref, g)
            accumulate(kb + 1, lg2_ref, p2_ref, g)
        return 0
    lax.fori_loop(0, nkb // 2, attn_pair, 0)

    @pl.when(nkb % 2 == 1)
    def _():
        for g in range(A_KV_HEADS):
            accumulate(nkb - 1, lg_ref, p_ref, g)

    low64 = lax.broadcasted_iota(I32, (tq, LANES), 1) < 64
    for j in range(A_HEADS // 2):
        g, r = (2 * j) // A_REP, (2 * j) % A_REP
        a0 = acc_ref[g, r * tq:(r + 1) * tq, :]
        a1 = acc_ref[g, (r + 1) * tq:(r + 2) * tq, :]
        lo = a0 / a0[:, 64:65]
        hi = pltpu.roll(a1, 64, 1) / a1[:, 64:65]
        o_ref[:, j * LANES:(j + 1) * LANES] = jnp.where(low64, lo, hi).astype(BF16)


def _mix_a_prompt(p, b, t, tq, tk):
    nq = t // tq
    n_sel = min(TOPK_MAX, t // 4)
    nkb_max = t // tk
    nsub = t // KEY_BLOCK
    qspec = lambda w: pl.BlockSpec((tq, w), lambda bb, i: (bb * nq + i, 0))
    ktspec = lambda w: pl.BlockSpec((None, nsub, w, KEY_BLOCK), lambda bb, i: (bb, 0, 0, 0))
    return pl.pallas_call(
        functools.partial(_mix_a_prompt_kernel, tk=tk, n_sel=n_sel),
        grid=(b, nq),
        in_specs=[qspec(256), qspec(WI_PAD), ktspec(128), qspec(512), ktspec(256),
                  pl.BlockSpec((t, 256), lambda bb, i: (bb, 0))],
        out_specs=qspec(512),
        out_shape=jax.ShapeDtypeStruct((b * t, 512), BF16),
        scratch_shapes=[
            pltpu.VMEM((nkb_max, tq, tk), F32),
            pltpu.VMEM((tq, tk), F32),
            pltpu.VMEM((tq, tk), F32),
            pltpu.VMEM((A_KV_HEADS, A_REP * tq, LANES), BF16),
            pltpu.VMEM((A_KV_HEADS, A_REP * tq, tk), F32),
            pltpu.VMEM((A_KV_HEADS, A_REP * tq, tk), F32),
            pltpu.VMEM((A_KV_HEADS, A_REP * tq, tk), BF16),
            pltpu.VMEM((A_KV_HEADS, A_REP * tq, tk), BF16),
            pltpu.VMEM((A_KV_HEADS, A_REP * tq, 1), F32),
            pltpu.VMEM((A_KV_HEADS, A_REP * tq, 1), F32),
            pltpu.VMEM((A_KV_HEADS, A_REP * tq, LANES), F32),
        ],
        compiler_params=_params("arbitrary", "arbitrary"),
        name="mix_a_prompt",
    )(p["qi"], p["wi"], p["kidT"], p["qa"], p["kadT"], p["vad"])


def _softplus(z):
    return jnp.maximum(z, 0.0) + jnp.log(1.0 + jnp.exp(-jnp.abs(z)))


def _mix_b_prompt_kernel(q_ref, k_ref, v_ref, tri_ref, o_ref, r_ref, acc_ref):
    tq = q_ref.shape[0]
    i = pl.program_id(2)
    lane = lax.broadcasted_iota(I32, (1, LANES), 1)
    half_mask = [jnp.where(lane < 64, 1.0, 0.0).astype(BF16), jnp.where(lane >= 64, 1.0, 0.0).astype(BF16)]
    q = q_ref[...]
    qs = jnp.concatenate([q * half_mask[0], q * half_mask[1]], axis=0)
    tri = tri_ref[...]
    r_ref[...] = jnp.zeros(r_ref.shape, F32)
    acc_ref[...] = jnp.zeros(acc_ref.shape, F32)

    def block(kb, diag):
        start = pl.multiple_of(kb * tq, tq)
        z = _dot(qs, k_ref[kb])
        sp = _softplus(z)
        log_keep = -sp
        if diag:
            local_q = lax.broadcasted_iota(I32, (2 * tq, tq), 0) & (tq - 1)
            earlier = lax.broadcasted_iota(I32, (2 * tq, tq), 1) < local_q
            log_keep = jnp.where(earlier, log_keep, 0.0)
        lh, ll = _split(log_keep)
        suffix = _dot(jnp.concatenate([lh, ll], axis=0), tri)
        between = suffix[0:2 * tq, :] + suffix[2 * tq:4 * tq, :] + r_ref[...]
        att = jnp.exp(z - sp + between)
        if diag:
            att = jnp.where(earlier, att, 0.0)
        acc_ref[...] += _dot(att.astype(BF16), v_ref[pl.ds(start, tq), :])
        r_ref[...] += jnp.sum(log_keep, axis=1, keepdims=True)

    def live():
        return jnp.where(jnp.max(r_ref[...]) >= STICK_CUTOFF, 1, 0)

    block(i, True)

    def body(c):
        block(i - 1 - c[0], False)
        return c[0] + 1, live()
    lax.while_loop(lambda c: (c[0] < i) & (c[1] > 0), body, (jnp.int32(0), live()))

    low64 = lax.broadcasted_iota(I32, (tq, LANES), 1) < 64
    o_ref[...] = jnp.where(low64, acc_ref[0:tq, :], acc_ref[tq:2 * tq, :]).astype(BF16)


def _strict_lower(n):
    j = np.arange(n)[:, None]
    s = np.arange(n)[None, :]
    return jnp.asarray((j > s).astype(np.float32), dtype=BF16)


def _mix_b_prompt(p, b, t, tq):
    assert tq == KEY_BLOCK
    nq = t // tq
    npair = B_HEADS // 2
    return pl.pallas_call(
        _mix_b_prompt_kernel,
        grid=(b, npair, nq),
        in_specs=[
            pl.BlockSpec((tq, LANES), lambda bb, j, i: (bb * nq + i, j)),
            pl.BlockSpec((None, nq, LANES, KEY_BLOCK), lambda bb, j, i: (bb, 0, j, 0)),
            pl.BlockSpec((t, LANES), lambda bb, j, i: (bb, j)),
            pl.BlockSpec((tq, tq), lambda bb, j, i: (0, 0)),
        ],
        out_specs=pl.BlockSpec((tq, LANES), lambda bb, j, i: (bb * nq + i, j)),
        out_shape=jax.ShapeDtypeStruct((b * t, B_HEADS * HEAD_DIM), BF16),
        scratch_shapes=[pltpu.VMEM((2 * tq, 1), F32), pltpu.VMEM((2 * tq, LANES), F32)],
        compiler_params=_params("arbitrary", "arbitrary", "arbitrary"),
        name="mix_b_prompt",
    )(p["qb"], p["kbT"], p["vbh"], _strict_lower(tq))


def _mix_a_sample_kernel(qi_ref, wi_ref, cki_ref, nki_ref, qa_ref, cka_ref, cva_ref, nka_ref, nva_ref, o_ref,
                         sc_ref, pad_ref, lg_ref, *, tk, n_sel):
    t = wi_ref.shape[0]
    past = cki_ref.shape[1]
    nkb = past // tk

    wi = wi_ref[...]
    qi = qi_ref[...]
    col = lax.broadcasted_iota(I32, (t, tk), 1)

    def scores(kT):
        d = _dot(qi, kT)
        s = jnp.zeros((t, tk), F32)
        for h in range(IDX_HEADS):
            s = s + jnp.maximum(d[h * t:(h + 1) * t, :], 0.0) * wi[:, h:h + 1]
        return jnp.where(jnp.abs(s) < MIN_NORMAL, 0.0, s)

    for kb in range(nkb):
        sc_ref[kb] = scores(cki_ref[:, kb * tk:(kb + 1) * tk].astype(BF16))
    pad_ref[...] = jnp.zeros(pad_ref.shape, F32)
    pad_ref[0, :, 0:t] = nki_ref[...]
    s_new = scores(pad_ref[0].astype(BF16))
    q_chunk = (past + lax.broadcasted_iota(I32, (t, 1), 0)) >> 6
    admissible = (col < t) & (((past + col) >> 6) <= q_chunk)
    sc_ref[nkb] = jnp.where(admissible, s_new, -jnp.inf)

    thr = _select_top(sc_ref, nkb + 1, t, tk, n_sel)

    thr_rep = jnp.concatenate([thr] * A_REP, axis=0)
    pad_ref[:, :, 0:t] = nka_ref[...]
    knew = pad_ref[...].astype(BF16)
    pad_ref[:, :, 0:t] = nva_ref[...]
    vnew = pad_ref[...].astype(BF16)

    def key_of(kb, g):
        return knew[g] if kb == nkb else cka_ref[g, :, kb * tk:(kb + 1) * tk].astype(BF16)

    def value_of(kb, g):
        return vnew[g] if kb == nkb else cva_ref[g, :, kb * tk:(kb + 1) * tk].astype(BF16)

    rows = A_REP * t
    running = [jnp.full((rows, LANES), MASKED, F32) for _ in range(A_KV_HEADS)]
    for kb in range(nkb + 1):
        sel = jnp.concatenate([sc_ref[kb]] * A_REP, axis=0) >= thr_rep
        for g in range(A_KV_HEADS):
            logits = jnp.where(sel, _dot(qa_ref[g], key_of(kb, g)), MASKED)
            lg_ref[g, kb] = logits
            running[g] = jnp.maximum(running[g], _lane_fold(logits, jnp.maximum))
    for g in range(A_KV_HEADS):
        m = jnp.max(running[g], axis=1, keepdims=True)
        total = jnp.zeros((rows, LANES), F32)
        acc = jnp.zeros((rows, HEAD_DIM), F32)
        for kb in range(nkb + 1):
            p = jnp.exp2(lg_ref[g, kb] - m)
            total = total + _lane_fold(p)
            acc = acc + _dot_nt(p.astype(BF16), value_of(kb, g))
        o_ref[g] = acc / jnp.sum(total, axis=1, keepdims=True)


def _mix_a_sample(qi_hm, wi, cache_idx_kT, nikT, qa_hm, cache_a_kT, cache_a_vT, nakT, navT, layer, tk):
    b, t, _ = wi.shape
    past = cache_idx_kT.shape[3]
    assert past % tk == 0
    n_sel = min(TOPK_MAX, (past + t) // 4)
    rows = A_REP * t
    per_b = lambda *blk: pl.BlockSpec((None,) + blk, lambda bb: (bb,) + (0,) * len(blk))
    kv_cache = pl.BlockSpec((None, None, A_KV_HEADS, HEAD_DIM, past), lambda bb: (layer, bb, 0, 0, 0))
    return pl.pallas_call(
        functools.partial(_mix_a_sample_kernel, tk=tk, n_sel=n_sel),
        grid=(b,),
        in_specs=[per_b(IDX_HEADS * t, IDX_DIM), per_b(t, WI_PAD),
                  pl.BlockSpec((None, None, IDX_DIM, past), lambda bb: (layer, bb, 0, 0)), per_b(IDX_DIM, t),
                  per_b(A_KV_HEADS, rows, HEAD_DIM), kv_cache, kv_cache,
                  per_b(A_KV_HEADS, HEAD_DIM, t), per_b(A_KV_HEADS, HEAD_DIM, t)],
        out_specs=per_b(A_KV_HEADS, rows, HEAD_DIM),
        out_shape=jax.ShapeDtypeStruct((b, A_KV_HEADS, rows, HEAD_DIM), F32),
        scratch_shapes=[
            pltpu.VMEM((past // tk + 1, t, tk), F32),
            pltpu.VMEM((A_KV_HEADS, HEAD_DIM, tk), F32),
            pltpu.VMEM((A_KV_HEADS, past // tk + 1, rows, tk), F32),
        ],
        compiler_params=_params("arbitrary"),
        name="mix_a_sample",
    )(qi_hm, wi, cache_idx_kT, nikT, qa_hm, cache_a_kT, cache_a_vT, nakT, navT)


def _mix_b_sample_kernel(q_ref, nk_ref, nv_ref, tri_ref, ck_hbm, cv_hbm, o_ref,
                         kbuf, vbuf, sem, r_ref, acc_ref, pad_ref, *, layer, tkc):
    b = pl.program_id(0)
    heads, t, _ = q_ref.shape
    nblk = ck_hbm.shape[4] // tkc
    npad = pad_ref.shape[3]

    def copies(j, slot):
        cols = pl.ds(pl.multiple_of(j * tkc, tkc), tkc)
        return (pltpu.make_async_copy(ck_hbm.at[layer, b, :, :, cols], kbuf.at[slot], sem.at[0, slot]),
                pltpu.make_async_copy(cv_hbm.at[layer, b, :, :, cols], vbuf.at[slot], sem.at[1, slot]))

    def block(k_of, v_of, nk, earlier):
        z = jnp.concatenate([_dot(q_ref[h], k_of(h)) for h in range(heads)], axis=0)
        sp = _softplus(z)
        log_keep = -sp
        if earlier is not None:
            log_keep = jnp.where(earlier, log_keep, 0.0)
        lh, ll = _split(log_keep)
        suffix = _dot(jnp.concatenate([lh, ll], axis=0), tri_ref[0:nk, 0:nk])
        rows = heads * t
        att = jnp.exp(z - sp + suffix[0:rows, :] + suffix[rows:2 * rows, :] + r_ref[...])
        if earlier is not None:
            att = jnp.where(earlier, att, 0.0)
        att = att.astype(BF16)
        for h in range(heads):
            acc_ref[h] += _dot_nt(att[h * t:(h + 1) * t, :], v_of(h))
        r_ref[...] += jnp.sum(log_keep, axis=1, keepdims=True)

    def live():
        return jnp.where(jnp.max(r_ref[...]) >= STICK_CUTOFF, 1, 0)

    for c in copies(nblk - 1, 0):
        c.start()

    r_ref[...] = jnp.zeros(r_ref.shape, F32)
    acc_ref[...] = jnp.zeros(acc_ref.shape, F32)
    pad_ref[...] = jnp.zeros(pad_ref.shape, F32)
    pad_ref[0, :, :, 0:t] = nk_ref[...]
    pad_ref[1, :, :, 0:t] = nv_ref[...]
    key_j = lax.broadcasted_iota(I32, (heads * t, npad), 1)
    query_i = lax.broadcasted_iota(I32, (heads * t, npad), 0) & (t - 1)
    block(lambda h: pad_ref[0, h].astype(BF16), lambda h: pad_ref[1, h].astype(BF16), npad, key_j < query_i)

    for c in copies(nblk - 1, 0):
        c.wait()

    def body(c):
        n = c[0]
        slot = n & 1
        block(lambda h: kbuf[slot, h].astype(BF16), lambda h: vbuf[slot, h].astype(BF16), tkc, None)
        alive = live()

        @pl.when((n + 1 < nblk) & (alive > 0))
        def _():
            nxt = copies(nblk - 2 - n, 1 - slot)
            for cp in nxt:
                cp.start()
            for cp in nxt:
                cp.wait()
        return n + 1, alive
    lax.while_loop(lambda c: (c[0] < nblk) & (c[1] > 0), body, (jnp.int32(0), live()))

    o_ref[...] = jnp.concatenate([acc_ref[h] for h in range(heads)], axis=1)


def _mix_b_sample(q, cache_kT, cache_vT, nkT, nvT, layer, tkc):
    b, heads, t, d = q.shape
    assert t & (t - 1) == 0 and heads * t == LANES and tkc >= LANES
    assert cache_kT.shape[4] % tkc == 0
    per_b = lambda *blk: pl.BlockSpec((None,) + blk, lambda bb: (bb,) + (0,) * len(blk))
    tri = _strict_lower(tkc)
    buf = (2, heads, d, tkc)
    return pl.pallas_call(
        functools.partial(_mix_b_sample_kernel, layer=layer, tkc=tkc),
        grid=(b,),
        in_specs=[per_b(heads, t, d), per_b(heads, d, t), per_b(heads, d, t), pl.BlockSpec((tkc, tkc), lambda bb: (0, 0)),
                  pl.BlockSpec(memory_space=pl.ANY), pl.BlockSpec(memory_space=pl.ANY)],
        out_specs=per_b(t, heads * d),
        out_shape=jax.ShapeDtypeStruct((b, t, heads * d), F32),
        scratch_shapes=[pltpu.VMEM(buf, F32), pltpu.VMEM(buf, F32), pltpu.SemaphoreType.DMA((2, 2)),
                        pltpu.VMEM((heads * t, 1), F32), pltpu.VMEM((heads, t, d), F32),
                        pltpu.VMEM((2, heads, d, LANES), F32)],
        compiler_params=_params("arbitrary"),
        name="mix_b_sample",
    )(q, nkT, nvT, tri, cache_kT, cache_vT)


def _merge_kernel(x_ref, oa_ref, ob_ref, gate_ref, g1_ref, sh2_ref, sc2_ref, n2_ref,
                  wa_ref, wb_ref, wo_ref, wr_ref, br_ref, x1_o, h2_o, rg_o):
    nb, tr, d = x_ref.shape
    tm = nb * tr
    a = _dot(oa_ref[...], wa_ref[...])
    bb = _dot(ob_ref[...], wb_ref[...])
    merged = gate_ref[:, 0:d] * a + gate_ref[:, d:2 * d] * bb
    mix = _dot(merged.astype(BF16), wo_ref[...])
    x1 = x_ref[...] + g1_ref[...] * mix.reshape(nb, tr, d)
    x1_o[...] = x1
    ms = jnp.mean(x1 * x1, axis=-1, keepdims=True)
    h2 = ((x1 * lax.rsqrt(ms + RMS_EPS) * n2_ref[...]) * (1.0 + sc2_ref[...]) + sh2_ref[...]).reshape(tm, d)
    h2_o[...] = h2.astype(BF16)

    logits = _dot3(h2, wr_ref[...])
    e = jnp.exp(logits - jnp.max(logits, axis=1, keepdims=True))
    aff = e / jnp.sum(e, axis=1, keepdims=True)
    biased = aff + br_ref[...]
    col = [biased[:, j:j + 1] for j in range(N_EXPERTS)]
    best_val = None
    for g in range(N_GROUPS):
        c = col[g * EXP_PER_GROUP:(g + 1) * EXP_PER_GROUP]
        score = None
        for u in range(EXP_PER_GROUP):
            for w in range(u + 1, EXP_PER_GROUP):
                pair = c[u] + c[w]
                score = pair if score is None else jnp.maximum(score, pair)
        if best_val is None:
            best_val, best_g = score, jnp.zeros((tm, 1), I32)
        else:
            upd = score > best_val
            best_g = jnp.where(upd, g, best_g)
            best_val = jnp.where(upd, score, best_val)
    eidx = lax.broadcasted_iota(I32, (tm, N_EXPERTS), 1)
    cand = jnp.where((eidx >> 2) == best_g, biased, -jnp.inf)
    m1 = jnp.max(cand, axis=1, keepdims=True)
    i1 = jnp.min(jnp.where(cand == m1, eidx, N_EXPERTS), axis=1, keepdims=True)
    cand2 = jnp.where(eidx == i1, -jnp.inf, cand)
    m2 = jnp.max(cand2, axis=1, keepdims=True)
    i2 = jnp.min(jnp.where(cand2 == m2, eidx, N_EXPERTS), axis=1, keepdims=True)
    w1 = jnp.sum(jnp.where(eidx == i1, aff, 0.0), axis=1, keepdims=True)
    w2 = jnp.sum(jnp.where(eidx == i2, aff, 0.0), axis=1, keepdims=True)
    tot = w1 + w2
    rg_o[...] = jnp.where(eidx == i1, w1 / tot, 0.0) + jnp.where(eidx == i2, w2 / tot, 0.0)


def _merge(x, oa, ob, gates, mod, layer, boff, n2, wa, wb, wo, wr, br, nb, tr):
    b, t, d = x.shape
    tpb = t // tr
    tm = nb * tr
    rows = b * t
    grid = (b // nb) * tpb

    def modspec(which):
        return pl.BlockSpec((None, None, nb, 1, d), lambda i: (layer, which, boff // nb + i // tpb, 0, 0))

    xspec = pl.BlockSpec((nb, tr, d), lambda i: (i // tpb, i % tpb, 0))
    row = lambda w: pl.BlockSpec((tm, w), lambda i: (i, 0))
    lw = lambda r, c: pl.BlockSpec((None, r, c), lambda i: (layer, 0, 0))
    return pl.pallas_call(
        _merge_kernel, grid=(grid,),
        in_specs=[xspec, row(512), row(512), row(2 * d), modspec(2), modspec(3), modspec(4), lw(1, d),
                  lw(512, d), lw(512, d), lw(d, d),
                  pl.BlockSpec((d, N_EXPERTS), lambda i: (0, 0)), pl.BlockSpec((1, N_EXPERTS), lambda i: (0, 0))],
        out_specs=[xspec, row(d), row(N_EXPERTS)],
        out_shape=[jax.ShapeDtypeStruct((b, t, d), F32), jax.ShapeDtypeStruct((rows, d), BF16),
                   jax.ShapeDtypeStruct((rows, N_EXPERTS), F32)],
        compiler_params=_params("arbitrary"), name="merge",
    )(x, oa, ob, gates, mod, mod, mod, n2, wa, wb, wo, wr, br)


def _moe_kernel(x_ref, h_ref, rg_ref, g2_ref, wg_ref, wu_ref, wd_ref, o_ref, acc_ref):
    nb, tr, d = x_ref.shape
    tm = nb * tr
    e = pl.program_id(1)

    @pl.when(e == 0)
    def _():
        acc_ref[...] = jnp.zeros(acc_ref.shape, F32)

    h = h_ref[...]
    gate = _dot(h, wg_ref[...])
    up = _dot(h, wu_ref[...])
    hidden = (gate * jax.nn.sigmoid(gate)) * up
    y = _dot(hidden.astype(BF16), wd_ref[...])
    eidx = lax.broadcasted_iota(I32, (tm, N_EXPERTS), 1)
    w = jnp.sum(jnp.where(eidx == e, rg_ref[...], 0.0), axis=1, keepdims=True)
    acc_ref[...] += w * y

    @pl.when(e == N_EXPERTS - 1)
    def _():
        o_ref[...] = x_ref[...] + g2_ref[...] * acc_ref[...].reshape(nb, tr, d)


def _moe(x1, h2, rg, mod, layer, boff, wg, wu, wd, nb, tr):
    b, t, d = x1.shape
    tpb = t // tr
    tm = nb * tr
    grid = (b // nb) * tpb
    de = wg.shape[-1]
    xspec = pl.BlockSpec((nb, tr, d), lambda i, e: (i // tpb, i % tpb, 0))
    return pl.pallas_call(
        _moe_kernel, grid=(grid, N_EXPERTS),
        in_specs=[xspec, pl.BlockSpec((tm, d), lambda i, e: (i, 0)), pl.BlockSpec((tm, N_EXPERTS), lambda i, e: (i, 0)),
                  pl.BlockSpec((None, None, nb, 1, d), lambda i, e: (layer, 5, boff // nb + i // tpb, 0, 0)),
                  pl.BlockSpec((None, None, d, de), lambda i, e: (layer, e, 0, 0)),
                  pl.BlockSpec((None, None, d, de), lambda i, e: (layer, e, 0, 0)),
                  pl.BlockSpec((None, None, de, d), lambda i, e: (layer, e, 0, 0))],
        out_specs=xspec,
        out_shape=jax.ShapeDtypeStruct((b, t, d), F32),
        scratch_shapes=[pltpu.VMEM((tm, d), F32)],
        compiler_params=_params("arbitrary", "arbitrary"), name="moe",
    )(x1, h2, rg, mod, wg, wu, wd)


def _rope_tables(pos):
    inv_freq = ROPE_THETA ** (-jnp.arange(0, HEAD_DIM, 2, dtype=F32) / HEAD_DIM)
    ang = pos.astype(F32)[:, None] * inv_freq[None, :]
    c, s = jnp.cos(ang), jnp.sin(ang)
    cos = jnp.concatenate([c, c, c, c], axis=1)
    sin = jnp.concatenate([-s, s, -s, s], axis=1)
    return cos, sin


def _widen_w_in(w_in):
    sizes = (512, 128, 128, 256, 64, 4, 512, 512, 512, 2048)
    pts = [int(v) for v in np.cumsum(sizes)[:-1]]
    qa, ka, va, qi, ki, wi, qb, kb, vb, gl = jnp.split(w_in, pts, axis=-1)
    dup = lambda w: jnp.concatenate([w[..., 0:64], w[..., 0:64], w[..., 64:128], w[..., 64:128]], axis=-1)
    wi_pad = jnp.pad(wi, ((0, 0), (0, 0), (0, LANES - wi.shape[-1])))
    ext = jnp.concatenate([qa, dup(ka), dup(va), qi, ki, ki, wi_pad, qb, kb, vb, gl], axis=-1)
    assert ext.shape[-1] == N_EXT
    return ext.astype(BF16)


def kernel(x_prompt, x_sample, cache_a_k, cache_a_v, cache_idx_k, cache_b_k, cache_b_v, c_prompt, c_sample,
           norm1_g, norm2_g, w_ada, b_ada, w_in, qn_g, kn_g, w_branch_a, w_branch_b, w_out, w_router, b_router,
           w_exp_gate, w_exp_up, w_exp_down):
    depth = w_in.shape[0]
    bp, tp, d = x_prompt.shape
    bs, ts, _ = x_sample.shape
    past = cache_a_k.shape[2]

    nc = -(-(bs + bp) // 8) * 8
    c_all = jnp.concatenate([c_sample, c_prompt, jnp.zeros((nc - bs - bp, d), F32)], axis=0)
    mod = _ada(c_all, w_ada, b_ada).reshape(depth, 6, nc, 1, d)
    boff_s, boff_p = 0, bs

    w_ext = _widen_w_in(w_in)
    wa, wb, wo = w_branch_a.astype(BF16), w_branch_b.astype(BF16), w_out.astype(BF16)
    wg, wu, wd = w_exp_gate.astype(BF16), w_exp_up.astype(BF16), w_exp_down.astype(BF16)
    n1 = norm1_g.reshape(depth, 1, d)
    n2 = norm2_g.reshape(depth, 1, d)
    qn = jnp.tile(qn_g, (1, 2)).reshape(depth, 1, LANES)
    kn = jnp.tile(kn_g, (1, 2)).reshape(depth, 1, LANES)
    seg = jnp.asarray(np.kron(np.eye(2), np.ones((64, 64))).astype(np.float32), dtype=BF16)
    br = b_router.reshape(1, N_EXPERTS)

    cos_p, sin_p = _rope_tables(jnp.arange(tp, dtype=I32))
    cos_s, sin_s = _rope_tables(past + jnp.arange(ts, dtype=I32))
    cos_s, sin_s = jnp.tile(cos_s, (bs, 1)), jnp.tile(sin_s, (bs, 1))

    tr_p = min(256, tp)
    tq_a = min(256, tp)
    tk_a = min(512, tp)
    tq_b = min(256, tp)
    tm_moe = min(1024, tp)
    tk_s = min(512, past)

    cache_b_kT = jnp.transpose(cache_b_k, (0, 1, 3, 4, 2))
    cache_b_vT = jnp.transpose(cache_b_v, (0, 1, 3, 4, 2))
    cache_a_kT = jnp.transpose(cache_a_k, (0, 1, 3, 4, 2))
    cache_a_vT = jnp.transpose(cache_a_v, (0, 1, 3, 4, 2))
    cache_idx_kT = jnp.transpose(cache_idx_k, (0, 1, 3, 2))

    xp, xs = x_prompt, x_sample
    rows_p, rows_s = [], []
    for l in range(depth):
        p = _in_proj(xp, mod, l, boff_p, n1, w_ext, cos_p, sin_p, qn, kn, seg, 1, tr_p, True)
        oa = _mix_a_prompt(p, bp, tp, tq_a, tk_a)
        ob = _mix_b_prompt(p, bp, tp, tq_b)
        x1, h2, rg = _merge(xp, oa, ob, p["gates"], mod, l, boff_p, n2, wa, wb, wo, w_router, br, 1, tr_p)
        xp = _moe(x1, h2, rg, mod, l, boff_p, wg, wu, wd, 1, tm_moe)
        rows_p.append(p)

        s = _in_proj(xs, mod, l, boff_s, n1, w_ext, cos_s, sin_s, qn, kn, seg, bs, ts, False)
        qi_hm = s["qi"].reshape(bs, ts, IDX_HEADS, IDX_DIM).transpose(0, 2, 1, 3).reshape(bs, IDX_HEADS * ts, IDX_DIM)
        qa5 = s["qa"].reshape(bs, ts, A_KV_HEADS, A_REP, HEAD_DIM).transpose(0, 2, 3, 1, 4)
        qa5 = qa5.reshape(bs, A_KV_HEADS, A_REP * ts, HEAD_DIM)
        kv_major = lambda a: a.reshape(bs, ts, A_KV_HEADS, HEAD_DIM).transpose(0, 2, 3, 1)
        oa_s = _mix_a_sample(qi_hm, s["wi"].reshape(bs, ts, WI_PAD), cache_idx_kT,
                             s["nik"].reshape(bs, ts, IDX_DIM).transpose(0, 2, 1), qa5, cache_a_kT, cache_a_vT,
                             kv_major(s["nak"]), kv_major(s["nav"]), l, tk_s)
        oa_s = oa_s.reshape(bs, A_KV_HEADS, A_REP, ts, HEAD_DIM).transpose(0, 3, 1, 2, 4).reshape(bs * ts, 512)
        heads_first = lambda a: a.reshape(bs, ts, B_HEADS, HEAD_DIM).transpose(0, 2, 1, 3)
        feature_major = lambda a: a.reshape(bs, ts, B_HEADS, HEAD_DIM).transpose(0, 2, 3, 1)
        ob_s = _mix_b_sample(heads_first(s["qb"]), cache_b_kT, cache_b_vT, feature_major(s["nbk"]),
                             feature_major(s["nbv"]), l, min(256, past))
        x1s, h2s, rgs = _merge(xs, oa_s.astype(BF16), ob_s.reshape(bs * ts, 512).astype(BF16), s["gates"], mod, l,
                               boff_s, n2, wa, wb, wo, w_router, br, bs, ts)
        xs = _moe(x1s, h2s, rgs, mod, l, boff_s, wg, wu, wd, bs, ts)
        rows_s.append(s)

    def stack(rows, name, b, t, tail):
        return jnp.stack([r[name].reshape((b, t) + tail) for r in rows], axis=0)

    def stack_t(rows, name, tail):
        a = jnp.stack([r[name] for r in rows], axis=0).reshape((depth, bp) + tail + (tp,))
        return jnp.moveaxis(a, -1, 2)

    kv = (A_KV_HEADS, HEAD_DIM)
    bh = (B_HEADS, HEAD_DIM)
    return (xp, xs,
            stack_t(rows_p, "nak", kv), stack_t(rows_p, "nav", kv), stack_t(rows_p, "nik", (IDX_DIM,)),
            stack_t(rows_p, "nbk", bh), stack_t(rows_p, "nbv", bh),
            stack(rows_s, "nak", bs, ts, kv), stack(rows_s, "nav", bs, ts, kv), stack(rows_s, "nik", bs, ts, (IDX_DIM,)),
            stack(rows_s, "nbk", bs, ts, bh), stack(rows_s, "nbv", bs, ts, bh))
```

```python
import functools

import jax
import jax.numpy as jnp
import numpy as np
from jax import lax
from jax.experimental import pallas as pl
from jax.experimental.pallas import tpu as pltpu

F32 = jnp.float32
BF16 = jnp.bfloat16
I32 = jnp.int32

HEAD_DIM = 64
A_HEADS = 8
A_KV_HEADS = 2
A_REP = A_HEADS // A_KV_HEADS
IDX_HEADS = 4
IDX_DIM = 64
B_HEADS = 8
CHUNK = 64
TOPK_MAX = 256
ROPE_THETA = 10000.0
N_EXPERTS = 16
N_GROUPS = 4
EXP_PER_GROUP = N_EXPERTS // N_GROUPS
RMS_EPS = 1e-6
LANES = 128
LOG2_E = 1.4426950408889634
A_Q_SCALE = HEAD_DIM ** -0.5 * LOG2_E
NEG_BIG = -1e30
MASKED = -3e30
STICK_CUTOFF = -120.0
INT_MIN = -2147483648
KEY_NEG_INF = -2139095041
MIN_NORMAL_BITS = 0x00800000
MIN_NORMAL = 2.0 ** -126
FLOAT_LOWEST = -3.4028234663852886e38
SEARCH_STEPS_PER_TEST = 4
SOFTMAX_STRIP_ROWS = 128

C_QA = 0
C_KAD = 512
C_VAD = 768
C_QI = 1024
C_KID = 1280
C_WI = 1408
C_QB = 1536
C_KB = 2048
C_VB = 2560
C_GL = 3072
N_EXT = 5120
WI_PAD = 8

VMEM_LIMIT = 56 * 1024 * 1024


def _dot(a, b):
    return jnp.dot(a, b, preferred_element_type=F32)


def _dot_nt(a, b):
    return lax.dot_general(a, b, (((1,), (1,)), ((), ())), preferred_element_type=F32)


def _split(a):
    hi = a.astype(BF16)
    lo = (a - hi.astype(F32)).astype(BF16)
    return hi, lo


def _dot3(a, b):
    ah, al = _split(a)
    bh, bl = _split(b)
    return _dot(ah, bh) + _dot(al, bh) + _dot(ah, bl)


def _params(*sem):
    return pltpu.CompilerParams(dimension_semantics=sem, vmem_limit_bytes=VMEM_LIMIT)


def _ada_kernel(c_ref, w_ref, b_ref, o_ref):
    c = c_ref[...]
    s = c * jax.nn.sigmoid(c)
    o_ref[...] = _dot3(s, w_ref[...]) + b_ref[...]


def _ada(c_all, w_ada, b_ada):
    depth, d, _ = w_ada.shape
    nc = c_all.shape[0]
    return pl.pallas_call(
        _ada_kernel,
        grid=(depth, 6),
        in_specs=[
            pl.BlockSpec((nc, d), lambda l, j: (0, 0)),
            pl.BlockSpec((None, d, d), lambda l, j: (l, 0, j)),
            pl.BlockSpec((None, 1, d), lambda l, j: (l, 0, j)),
        ],
        out_specs=pl.BlockSpec((None, None, nc, d), lambda l, j: (l, j, 0, 0)),
        out_shape=jax.ShapeDtypeStruct((depth, 6, nc, d), F32),
        compiler_params=_params("arbitrary", "arbitrary"),
        name="ada_mod",
    )(c_all, w_ada, b_ada.reshape(depth, 1, 6 * d))


def _rope(v, cos, sin, first_half):
    partner = jnp.where(first_half, pltpu.roll(v, LANES - 32, 1), pltpu.roll(v, 32, 1))
    return v * cos + partner * sin


_IN_PROJ_ROWS = (("qa", 512, BF16), ("qi", 256, BF16), ("wi", WI_PAD, F32), ("qb", 512, BF16), ("gates", 2048, F32))
_IN_PROJ_NEW_ROWS = (("nak", 128), ("nav", 128), ("nik", 64), ("nbk", 512), ("nbv", 512))
_IN_PROJ_PROMPT_ROWS = (("vad", 256, BF16), ("vbh", 512, BF16))
_IN_PROJ_PROMPT_KEYS = (("kadT", 256), ("kidT", 128), ("kbT", 512))
KEY_BLOCK = 256


def _in_proj_kernel(x_ref, sh_ref, sc_ref, g_ref, w_ref, cos_ref, sin_ref, qn_ref, kn_ref, seg_ref, *out_refs,
                    names):
    o = dict(zip(names, out_refs))
    prompt = "kadT" in o
    nb, tr, d = x_ref.shape
    tm = nb * tr
    x = x_ref[...]
    ms = jnp.mean(x * x, axis=-1, keepdims=True)
    h = (x * lax.rsqrt(ms + RMS_EPS) * g_ref[...]) * (1.0 + sc_ref[...]) + sh_ref[...]
    hb = h.reshape(tm, d).astype(BF16)

    def proj(lo, width):
        return _dot(hb, w_ref[:, lo:lo + width])

    cos = cos_ref[...]
    sin = sin_ref[...]
    seg = seg_ref[...]
    lane = lax.broadcasted_iota(I32, (tm, LANES), 1)
    first_half = (lane & 32) == 0
    low64 = lane < 64

    def head_norm(v, gain):
        v2 = v * v
        v2h, v2l = _split(v2)
        msq = (_dot(v2h, seg) + _dot(v2l, seg)) * (1.0 / HEAD_DIM)
        return v * lax.rsqrt(msq + RMS_EPS) * gain

    qn = qn_ref[...]
    kn = kn_ref[...]
    v = proj(C_QA, 512)
    for c in range(4):
        blk = _rope(head_norm(v[:, c * LANES:(c + 1) * LANES], qn), cos, sin, first_half)
        o["qa"][:, c * LANES:(c + 1) * LANES] = (blk * A_Q_SCALE).astype(BF16)
    v = proj(C_KAD, 256)
    kd = [_rope(head_norm(v[:, c * LANES:(c + 1) * LANES], kn), cos, sin, first_half) for c in range(2)]
    v = proj(C_VAD, 256)
    if prompt:
        for c in range(2):
            kT = kd[c].T
            o["kadT"][c * LANES:(c + 1) * LANES, :] = kT.astype(BF16)
            o["nak"][c * HEAD_DIM:(c + 1) * HEAD_DIM, :] = kT[0:HEAD_DIM, :]
            o["nav"][c * HEAD_DIM:(c + 1) * HEAD_DIM, :] = v[:, c * LANES:(c + 1) * LANES].T[0:HEAD_DIM, :]
        ones_col = jnp.where(lane == 64, 1.0, 0.0)
        for c in range(2):
            blk = jnp.where(low64, v[:, c * LANES:(c + 1) * LANES], ones_col)
            o["vad"][:, c * LANES:(c + 1) * LANES] = blk.astype(BF16)
    else:
        o["nak"][...] = jnp.where(low64, kd[0], kd[1])
        o["nav"][...] = jnp.where(low64, v[:, 0:LANES], v[:, LANES:2 * LANES])
    v = proj(C_QI, 256)
    for c in range(2):
        blk = _rope(v[:, c * LANES:(c + 1) * LANES], cos, sin, first_half)
        o["qi"][:, c * LANES:(c + 1) * LANES] = (blk * (IDX_DIM ** -0.5)).astype(BF16)
    v = _rope(proj(C_KID, LANES), cos, sin, first_half)
    if prompt:
        kT = v.T
        o["kidT"][...] = kT.astype(BF16)
        o["nik"][...] = kT[0:IDX_DIM, :]
    else:
        o["nik"][...] = v[:, 0:IDX_DIM]
    v = proj(C_WI, LANES)
    o["wi"][...] = v[:, 0:WI_PAD] * (IDX_HEADS ** -0.5)
    o["qb"][...] = (proj(C_QB, 512) * (HEAD_DIM ** -0.5)).astype(BF16)
    v = proj(C_KB, 512)
    if prompt:
        kT = v.T
        o["kbT"][...] = kT.astype(BF16)
        o["nbk"][...] = kT
    else:
        o["nbk"][...] = v
    v = proj(C_VB, 512)
    if prompt:
        o["vbh"][...] = v.astype(BF16)
        o["nbv"][...] = v.T
    else:
        o["nbv"][...] = v
    for c in range(4):
        o["gates"][:, c * 512:(c + 1) * 512] = jax.nn.sigmoid(proj(C_GL + c * 512, 512))


def _in_proj(x, mod, layer, boff, norm_g, w_ext, cos, sin, qn, kn, seg, nb, tr, prompt):
    b, t, d = x.shape
    tpb = t // tr
    tm = nb * tr
    rows = b * t
    grid = (b // nb) * tpb

    def modspec(which):
        return pl.BlockSpec((None, None, nb, 1, d), lambda i: (layer, which, boff // nb + i // tpb, 0, 0))

    in_specs = [
        pl.BlockSpec((nb, tr, d), lambda i: (i // tpb, i % tpb, 0)),
        modspec(0), modspec(1),
        pl.BlockSpec((None, 1, d), lambda i: (layer, 0, 0)),
        pl.BlockSpec((None, d, N_EXT), lambda i: (layer, 0, 0)),
        pl.BlockSpec((tm, LANES), lambda i: (i % tpb, 0)),
        pl.BlockSpec((tm, LANES), lambda i: (i % tpb, 0)),
        pl.BlockSpec((None, 1, LANES), lambda i: (layer, 0, 0)),
        pl.BlockSpec((None, 1, LANES), lambda i: (layer, 0, 0)),
        pl.BlockSpec((LANES, LANES), lambda i: (0, 0)),
    ]
    row_outs = _IN_PROJ_ROWS + (_IN_PROJ_PROMPT_ROWS if prompt else tuple((n, w, F32) for n, w in _IN_PROJ_NEW_ROWS))
    names = [n for n, _, _ in row_outs]
    out_specs = [pl.BlockSpec((tm, w), lambda i: (i, 0)) for _, w, _ in row_outs]
    out_shape = [jax.ShapeDtypeStruct((rows, w), dt) for _, w, dt in row_outs]
    if prompt:
        assert nb == 1 and tr == KEY_BLOCK
        for n, w in _IN_PROJ_PROMPT_KEYS:
            names.append(n)
            out_specs.append(pl.BlockSpec((None, None, w, KEY_BLOCK), lambda i: (i // tpb, i % tpb, 0, 0)))
            out_shape.append(jax.ShapeDtypeStruct((b, tpb, w, KEY_BLOCK), BF16))
        for n, w in _IN_PROJ_NEW_ROWS:
            names.append(n)
            out_specs.append(pl.BlockSpec((None, w, KEY_BLOCK), lambda i: (i // tpb, 0, i % tpb)))
            out_shape.append(jax.ShapeDtypeStruct((b, w, t), F32))
    outs = pl.pallas_call(
        functools.partial(_in_proj_kernel, names=tuple(names)),
        grid=(grid,), in_specs=in_specs, out_specs=out_specs, out_shape=out_shape,
        compiler_params=_params("arbitrary"), name="in_proj",
    )(x, mod, mod, norm_g, w_ext, cos, sin, qn, kn, seg)
    return dict(zip(names, outs))


def _candidate_float(key):
    bits = key ^ ((key >> 31) & 0x7FFFFFFF)
    bits = jnp.where((key > 0) & (key < MIN_NORMAL_BITS), MIN_NORMAL_BITS, bits)
    return lax.bitcast_convert_type(bits, F32)


def _lane_fold(m, op=jnp.add):
    acc = m[:, 0:LANES]
    for c in range(1, m.shape[1] // LANES):
        acc = op(acc, m[:, c * LANES:(c + 1) * LANES])
    return acc


def _select_top(sc_ref, cnt_ref, nkb, rows, tk, n_sel):
    def count(compare, c):
        strip = min(rows, SOFTMAX_STRIP_ROWS)
        c_b = [jnp.broadcast_to(c[r0:r0 + strip, :], (strip, LANES)) for r0 in range(0, rows, strip)]
        cnt_ref[...] = jnp.zeros(cnt_ref.shape, cnt_ref.dtype)

        def body(kb, _):
            for i, r0 in enumerate(range(0, rows, strip)):
                hits = None
                for j in range(tk // LANES):
                    h = jnp.where(compare(sc_ref[kb, r0:r0 + strip, j * LANES:(j + 1) * LANES], c_b[i]), 1, 0)
                    hits = h if hits is None else hits + h
                cnt_ref[r0:r0 + strip, :] += hits
            return 0
        lax.fori_loop(0, nkb, body, 0)
        return jnp.sum(cnt_ref[...], axis=1, keepdims=True)

    def step(it, v, hit):
        cand = v ^ jnp.left_shift(jnp.int32(1), 31 - it)
        cnt = count(jnp.greater_equal, _candidate_float(cand))
        ok = cnt >= n_sel
        return jnp.where(ok, cand, v), hit | jnp.where(ok & (cnt == n_sel), 1, 0)

    def steps(c):
        g, v, hit, _ = c
        for u in range(SEARCH_STEPS_PER_TEST):
            v, hit = step(g * SEARCH_STEPS_PER_TEST + u, v, hit)
        return g + 1, v, hit, jnp.min(hit)
    init = (jnp.int32(0), jnp.full((rows, 1), INT_MIN, I32), jnp.zeros((rows, 1), I32), jnp.int32(0))
    _, v, hit, all_hit = lax.while_loop(lambda c: (c[0] < 32 // SEARCH_STEPS_PER_TEST) & (c[3] == 0), steps, init)
    thr = jnp.where(v < KEY_NEG_INF, -jnp.inf, _candidate_float(v))

    @pl.when(all_hit == 0)
    def _():
        cnt_ge = count(jnp.greater_equal, thr)
        need = n_sel - count(jnp.greater, thr)
        tied = (thr > -jnp.inf) & (cnt_ge > n_sel)

        @pl.when(jnp.max(jnp.where(tied, 1, 0)) > 0)
        def _():
            r_i = lax.broadcasted_iota(I32, (tk, tk), 0)
            c_i = lax.broadcasted_iota(I32, (tk, tk), 1)
            upper = jnp.where(r_i <= c_i, 1.0, 0.0).astype(BF16)
            need_f = need.astype(F32)

            def body(kb, carry):
                s = sc_ref[kb]
                eq = s == thr
                prefix = _dot(jnp.where(eq, 1.0, 0.0).astype(BF16), upper) + carry
                sc_ref[kb] = jnp.where(eq & (prefix > need_f), -jnp.inf, s)
                return prefix[:, tk - 1:tk]
            lax.fori_loop(0, nkb, body, jnp.zeros((rows, 1), F32))

    return jnp.maximum(thr, FLOAT_LOWEST)


def _mix_a_prompt_kernel(qi_ref, wi_ref, kidT_ref, qa_ref, kadT_ref, vad_ref, o_ref,
                         sc_ref, cnt_ref, bias_ref, bias2_ref, qm_ref, lg_ref, lg2_ref, p_ref, p2_ref, m_ref, mn_ref,
                         acc_ref, *, tk, n_sel):
    tq = qi_ref.shape[0]
    nsb = tk // KEY_BLOCK
    i = pl.program_id(1)
    nkb = ((i + 1) * tq + tk - 1) // tk
    lane = lax.broadcasted_iota(I32, (1, LANES), 1)
    half_mask = [jnp.where(lane < 64, 1.0, 0.0).astype(BF16), jnp.where(lane >= 64, 1.0, 0.0).astype(BF16)]

    def stack_heads(src_ref, dst, first):
        for r in range(A_REP):
            h = first + r
            dst[r * tq:(r + 1) * tq, :] = src_ref[:, (h // 2) * LANES:(h // 2 + 1) * LANES] * half_mask[h % 2]

    wi = wi_ref[...]
    stack_heads(qi_ref, qm_ref.at[0], 0)
    q_chunk = (i * tq + lax.broadcasted_iota(I32, (tq, 1), 0)) >> 6

    def score_block(kb, _):
        for sb in range(nsb):
            dots = _dot(qm_ref[0], kidT_ref[kb * nsb + sb])
            s = jnp.zeros((tq, KEY_BLOCK), F32)
            for h in range(IDX_HEADS):
                s = s + jnp.maximum(dots[h * tq:(h + 1) * tq, :], 0.0) * wi[:, h:h + 1]
            k_chunk = (kb * tk + sb * KEY_BLOCK + lax.broadcasted_iota(I32, (1, KEY_BLOCK), 1)) >> 6
            s = jnp.where(k_chunk <= q_chunk, jnp.where(jnp.abs(s) < MIN_NORMAL, 0.0, s), -jnp.inf)
            sc_ref[kb, :, sb * KEY_BLOCK:(sb + 1) * KEY_BLOCK] = s
        return 0
    lax.fori_loop(0, nkb, score_block, 0)

    thr = _select_top(sc_ref, cnt_ref, nkb, tq, tk, n_sel)

    m_ref[...] = jnp.full(m_ref.shape, NEG_BIG, F32)
    acc_ref[...] = jnp.zeros(acc_ref.shape, F32)
    for g in range(A_KV_HEADS):
        stack_heads(qa_ref, qm_ref.at[g], g * A_REP)

    def masked_logits(kb, lg, bias, g):
        if g == 0:
            bias[...] = jnp.where(sc_ref[kb] >= thr, 0.0, MASKED)
        for sb in range(nsb):
            cols = slice(sb * KEY_BLOCK, (sb + 1) * KEY_BLOCK)
            logits = _dot(qm_ref[g], kadT_ref[kb * nsb + sb, g * LANES:(g + 1) * LANES, :])
            for r in range(A_REP):
                lg[g, r * tq:(r + 1) * tq, cols] = logits[r * tq:(r + 1) * tq, :] + bias[:, cols]

    def accumulate(kb, lg, p, g):
        start = pl.multiple_of(kb * tk, tk)
        for r0 in range(0, A_REP * tq, SOFTMAX_STRIP_ROWS):
            rows = slice(r0, r0 + SOFTMAX_STRIP_ROWS)
            logits = lg[g, rows, :]
            m_new = jnp.maximum(m_ref[g, rows, :], jnp.max(_lane_fold(logits, jnp.maximum), axis=1, keepdims=True))
            mn_ref[g, rows, :] = m_new
            p[g, rows, :] = jnp.exp2(logits - m_new).astype(BF16)
        vblk = vad_ref[pl.ds(start, tk), g * LANES:(g + 1) * LANES]
        pv = _dot(p[g], vblk)
        for r0 in range(0, A_REP * tq, SOFTMAX_STRIP_ROWS):
            rows = slice(r0, r0 + SOFTMAX_STRIP_ROWS)
            alpha = jnp.exp2(m_ref[g, rows, :] - mn_ref[g, rows, :])
            acc_ref[g, rows, :] = alpha * acc_ref[g, rows, :] + pv[r0:r0 + SOFTMAX_STRIP_ROWS, :]
            m_ref[g, rows, :] = mn_ref[g, rows, :]

    for g in range(A_KV_HEADS):
        masked_logits(0, lg_ref, bias_ref, g)

    def attn_pair(j, _):
        kb = 2 * j
        for g in range(A_KV_HEADS):
            masked_logits(kb + 1, lg2_ref, bias2_ref, g)
            accumulate(kb, lg_ref, p_ref, g)
        for g in range(A_KV_HEADS):
            masked_logits(jnp.minimum(kb + 2, nkb - 1), lg_ref, bias_ref, g)
            accumulate(kb + 1, lg2_ref, p2_ref, g)
        return 0
    lax.fori_loop(0, nkb // 2, attn_pair, 0)

    @pl.when(nkb % 2 == 1)
    def _():
        for g in range(A_KV_HEADS):
            accumulate(nkb - 1, lg_ref, p_ref, g)

    low64 = lax.broadcasted_iota(I32, (tq, LANES), 1) < 64
    for j in range(A_HEADS // 2):
        g, r = (2 * j) // A_REP, (2 * j) % A_REP
        a0 = acc_ref[g, r * tq:(r + 1) * tq, :]
        a1 = acc_ref[g, (r + 1) * tq:(r + 2) * tq, :]
        lo = a0 / a0[:, 64:65]
        hi = pltpu.roll(a1, 64, 1) / a1[:, 64:65]
        o_ref[:, j * LANES:(j + 1) * LANES] = jnp.where(low64, lo, hi).astype(BF16)


def _mix_a_prompt(p, b, t, tq, tk):
    nq = t // tq
    n_sel = min(TOPK_MAX, t // 4)
    nkb_max = t // tk
    nsub = t // KEY_BLOCK
    qspec = lambda w: pl.BlockSpec((tq, w), lambda bb, i: (bb * nq + i, 0))
    ktspec = lambda w: pl.BlockSpec((None, nsub, w, KEY_BLOCK), lambda bb, i: (bb, 0, 0, 0))
    return pl.pallas_call(
        functools.partial(_mix_a_prompt_kernel, tk=tk, n_sel=n_sel),
        grid=(b, nq),
        in_specs=[qspec(256), qspec(WI_PAD), ktspec(128), qspec(512), ktspec(256),
                  pl.BlockSpec((t, 256), lambda bb, i: (bb, 0))],
        out_specs=qspec(512),
        out_shape=jax.ShapeDtypeStruct((b * t, 512), BF16),
        scratch_shapes=[
            pltpu.VMEM((nkb_max, tq, tk), F32),
            pltpu.VMEM((tq, LANES), I32),
            pltpu.VMEM((tq, tk), F32),
            pltpu.VMEM((tq, tk), F32),
            pltpu.VMEM((A_KV_HEADS, A_REP * tq, LANES), BF16),
            pltpu.VMEM((A_KV_HEADS, A_REP * tq, tk), F32),
            pltpu.VMEM((A_KV_HEADS, A_REP * tq, tk), F32),
            pltpu.VMEM((A_KV_HEADS, A_REP * tq, tk), BF16),
            pltpu.VMEM((A_KV_HEADS, A_REP * tq, tk), BF16),
            pltpu.VMEM((A_KV_HEADS, A_REP * tq, 1), F32),
            pltpu.VMEM((A_KV_HEADS, A_REP * tq, 1), F32),
            pltpu.VMEM((A_KV_HEADS, A_REP * tq, LANES), F32),
        ],
        compiler_params=_params("arbitrary", "arbitrary"),
        name="mix_a_prompt",
    )(p["qi"], p["wi"], p["kidT"], p["qa"], p["kadT"], p["vad"])


def _softplus(z):
    return jnp.maximum(z, 0.0) + jnp.log(1.0 + jnp.exp(-jnp.abs(z)))


def _mix_b_prompt_kernel(q_ref, k_ref, v_ref, tri_ref, o_ref, r_ref, acc_ref):
    tq = q_ref.shape[0]
    i = pl.program_id(2)
    lane = lax.broadcasted_iota(I32, (1, LANES), 1)
    half_mask = [jnp.where(lane < 64, 1.0, 0.0).astype(BF16), jnp.where(lane >= 64, 1.0, 0.0).astype(BF16)]
    q = q_ref[...]
    qs = jnp.concatenate([q * half_mask[0], q * half_mask[1]], axis=0)
    tri = tri_ref[...]
    r_ref[...] = jnp.zeros(r_ref.shape, F32)
    acc_ref[...] = jnp.zeros(acc_ref.shape, F32)

    def block(kb, diag):
        start = pl.multiple_of(kb * tq, tq)
        z = _dot(qs, k_ref[kb])
        sp = _softplus(z)
        log_keep = -sp
        if diag:
            local_q = lax.broadcasted_iota(I32, (2 * tq, tq), 0) & (tq - 1)
            earlier = lax.broadcasted_iota(I32, (2 * tq, tq), 1) < local_q
            log_keep = jnp.where(earlier, log_keep, 0.0)
        lh, ll = _split(log_keep)
        suffix = _dot(jnp.concatenate([lh, ll], axis=0), tri)
        between = suffix[0:2 * tq, :] + suffix[2 * tq:4 * tq, :] + r_ref[...]
        att = jnp.exp(z - sp + between)
        if diag:
            att = jnp.where(earlier, att, 0.0)
        acc_ref[...] += _dot(att.astype(BF16), v_ref[pl.ds(start, tq), :])
        r_ref[...] += jnp.sum(log_keep, axis=1, keepdims=True)

    def live():
        return jnp.where(jnp.max(r_ref[...]) >= STICK_CUTOFF, 1, 0)

    block(i, True)

    def body(c):
        block(i - 1 - c[0], False)
        return c[0] + 1, live()
    lax.while_loop(lambda c: (c[0] < i) & (c[1] > 0), body, (jnp.int32(0), live()))

    low64 = lax.broadcasted_iota(I32, (tq, LANES), 1) < 64
    o_ref[...] = jnp.where(low64, acc_ref[0:tq, :], acc_ref[tq:2 * tq, :]).astype(BF16)


def _strict_lower(n):
    j = np.arange(n)[:, None]
    s = np.arange(n)[None, :]
    return jnp.asarray((j > s).astype(np.float32), dtype=BF16)


def _mix_b_prompt(p, b, t, tq):
    assert tq == KEY_BLOCK
    nq = t // tq
    npair = B_HEADS // 2
    return pl.pallas_call(
        _mix_b_prompt_kernel,
        grid=(b, npair, nq),
        in_specs=[
            pl.BlockSpec((tq, LANES), lambda bb, j, i: (bb * nq + i, j)),
            pl.BlockSpec((None, nq, LANES, KEY_BLOCK), lambda bb, j, i: (bb, 0, j, 0)),
            pl.BlockSpec((t, LANES), lambda bb, j, i: (bb, j)),
            pl.BlockSpec((tq, tq), lambda bb, j, i: (0, 0)),
        ],
        out_specs=pl.BlockSpec((tq, LANES), lambda bb, j, i: (bb * nq + i, j)),
        out_shape=jax.ShapeDtypeStruct((b * t, B_HEADS * HEAD_DIM), BF16),
        scratch_shapes=[pltpu.VMEM((2 * tq, 1), F32), pltpu.VMEM((2 * tq, LANES), F32)],
        compiler_params=_params("arbitrary", "arbitrary", "arbitrary"),
        name="mix_b_prompt",
    )(p["qb"], p["kbT"], p["vbh"], _strict_lower(tq))


def _mix_a_sample_kernel(qi_ref, wi_ref, cki_ref, nki_ref, qa_ref, cka_ref, cva_ref, nka_ref, nva_ref, o_ref,
                         sc_ref, cnt_ref, pad_ref, lg_ref, *, tk, n_sel):
    t = wi_ref.shape[0]
    past = cki_ref.shape[1]
    nkb = past // tk

    wi = wi_ref[...]
    qi = qi_ref[...]
    col = lax.broadcasted_iota(I32, (t, tk), 1)

    def scores(kT):
        d = _dot(qi, kT)
        s = jnp.zeros((t, tk), F32)
        for h in range(IDX_HEADS):
            s = s + jnp.maximum(d[h * t:(h + 1) * t, :], 0.0) * wi[:, h:h + 1]
        return jnp.where(jnp.abs(s) < MIN_NORMAL, 0.0, s)

    for kb in range(nkb):
        sc_ref[kb] = scores(cki_ref[:, kb * tk:(kb + 1) * tk].astype(BF16))
    pad_ref[...] = jnp.zeros(pad_ref.shape, F32)
    pad_ref[0, :, 0:t] = nki_ref[...]
    s_new = scores(pad_ref[0].astype(BF16))
    q_chunk = (past + lax.broadcasted_iota(I32, (t, 1), 0)) >> 6
    admissible = (col < t) & (((past + col) >> 6) <= q_chunk)
    sc_ref[nkb] = jnp.where(admissible, s_new, -jnp.inf)

    thr = _select_top(sc_ref, cnt_ref, nkb + 1, t, tk, n_sel)

    thr_rep = jnp.concatenate([thr] * A_REP, axis=0)
    pad_ref[:, :, 0:t] = nka_ref[...]
    knew = pad_ref[...].astype(BF16)
    pad_ref[:, :, 0:t] = nva_ref[...]
    vnew = pad_ref[...].astype(BF16)

    def key_of(kb, g):
        return knew[g] if kb == nkb else cka_ref[g, :, kb * tk:(kb + 1) * tk].astype(BF16)

    def value_of(kb, g):
        return vnew[g] if kb == nkb else cva_ref[g, :, kb * tk:(kb + 1) * tk].astype(BF16)

    rows = A_REP * t
    running = [jnp.full((rows, LANES), MASKED, F32) for _ in range(A_KV_HEADS)]
    for kb in range(nkb + 1):
        sel = jnp.concatenate([sc_ref[kb]] * A_REP, axis=0) >= thr_rep
        for g in range(A_KV_HEADS):
            logits = jnp.where(sel, _dot(qa_ref[g], key_of(kb, g)), MASKED)
            lg_ref[g, kb] = logits
            running[g] = jnp.maximum(running[g], _lane_fold(logits, jnp.maximum))
    for g in range(A_KV_HEADS):
        m = jnp.max(running[g], axis=1, keepdims=True)
        total = jnp.zeros((rows, LANES), F32)
        acc = jnp.zeros((rows, HEAD_DIM), F32)
        for kb in range(nkb + 1):
            p = jnp.exp2(lg_ref[g, kb] - m)
            total = total + _lane_fold(p)
            acc = acc + _dot_nt(p.astype(BF16), value_of(kb, g))
        o_ref[g] = acc / jnp.sum(total, axis=1, keepdims=True)


def _mix_a_sample(qi_hm, wi, cache_idx_kT, nikT, qa_hm, cache_a_kT, cache_a_vT, nakT, navT, layer, tk):
    b, t, _ = wi.shape
    past = cache_idx_kT.shape[3]
    assert past % tk == 0
    n_sel = min(TOPK_MAX, (past + t) // 4)
    rows = A_REP * t
    per_b = lambda *blk: pl.BlockSpec((None,) + blk, lambda bb: (bb,) + (0,) * len(blk))
    kv_cache = pl.BlockSpec((None, None, A_KV_HEADS, HEAD_DIM, past), lambda bb: (layer, bb, 0, 0, 0))
    return pl.pallas_call(
        functools.partial(_mix_a_sample_kernel, tk=tk, n_sel=n_sel),
        grid=(b,),
        in_specs=[per_b(IDX_HEADS * t, IDX_DIM), per_b(t, WI_PAD),
                  pl.BlockSpec((None, None, IDX_DIM, past), lambda bb: (layer, bb, 0, 0)), per_b(IDX_DIM, t),
                  per_b(A_KV_HEADS, rows, HEAD_DIM), kv_cache, kv_cache,
                  per_b(A_KV_HEADS, HEAD_DIM, t), per_b(A_KV_HEADS, HEAD_DIM, t)],
        out_specs=per_b(A_KV_HEADS, rows, HEAD_DIM),
        out_shape=jax.ShapeDtypeStruct((b, A_KV_HEADS, rows, HEAD_DIM), F32),
        scratch_shapes=[
            pltpu.VMEM((past // tk + 1, t, tk), F32),
            pltpu.VMEM((t, LANES), I32),
            pltpu.VMEM((A_KV_HEADS, HEAD_DIM, tk), F32),
            pltpu.VMEM((A_KV_HEADS, past // tk + 1, rows, tk), F32),
        ],
        compiler_params=_params("arbitrary"),
        name="mix_a_sample",
    )(qi_hm, wi, cache_idx_kT, nikT, qa_hm, cache_a_kT, cache_a_vT, nakT, navT)


def _mix_b_sample_kernel(q_ref, nk_ref, nv_ref, tri_ref, ck_hbm, cv_hbm, o_ref,
                         kbuf, vbuf, sem, r_ref, acc_ref, pad_ref, *, layer, tkc):
    b = pl.program_id(0)
    heads, t, _ = q_ref.shape
    nblk = ck_hbm.shape[4] // tkc
    npad = pad_ref.shape[3]

    def copies(j, slot):
        cols = pl.ds(pl.multiple_of(j * tkc, tkc), tkc)
        return (pltpu.make_async_copy(ck_hbm.at[layer, b, :, :, cols], kbuf.at[slot], sem.at[0, slot]),
                pltpu.make_async_copy(cv_hbm.at[layer, b, :, :, cols], vbuf.at[slot], sem.at[1, slot]))

    def block(k_of, v_of, nk, earlier):
        z = jnp.concatenate([_dot(q_ref[h], k_of(h)) for h in range(heads)], axis=0)
        sp = _softplus(z)
        log_keep = -sp
        if earlier is not None:
            log_keep = jnp.where(earlier, log_keep, 0.0)
        lh, ll = _split(log_keep)
        suffix = _dot(jnp.concatenate([lh, ll], axis=0), tri_ref[0:nk, 0:nk])
        rows = heads * t
        att = jnp.exp(z - sp + suffix[0:rows, :] + suffix[rows:2 * rows, :] + r_ref[...])
        if earlier is not None:
            att = jnp.where(earlier, att, 0.0)
        att = att.astype(BF16)
        for h in range(heads):
            acc_ref[h] += _dot_nt(att[h * t:(h + 1) * t, :], v_of(h))
        r_ref[...] += jnp.sum(log_keep, axis=1, keepdims=True)

    def live():
        return jnp.where(jnp.max(r_ref[...]) >= STICK_CUTOFF, 1, 0)

    for c in copies(nblk - 1, 0):
        c.start()

    r_ref[...] = jnp.zeros(r_ref.shape, F32)
    acc_ref[...] = jnp.zeros(acc_ref.shape, F32)
    pad_ref[...] = jnp.zeros(pad_ref.shape, F32)
    pad_ref[0, :, :, 0:t] = nk_ref[...]
    pad_ref[1, :, :, 0:t] = nv_ref[...]
    key_j = lax.broadcasted_iota(I32, (heads * t, npad), 1)
    query_i = lax.broadcasted_iota(I32, (heads * t, npad), 0) & (t - 1)
    block(lambda h: pad_ref[0, h].astype(BF16), lambda h: pad_ref[1, h].astype(BF16), npad, key_j < query_i)

    for c in copies(nblk - 1, 0):
        c.wait()

    def body(c):
        n = c[0]
        slot = n & 1
        block(lambda h: kbuf[slot, h].astype(BF16), lambda h: vbuf[slot, h].astype(BF16), tkc, None)
        alive = live()

        @pl.when((n + 1 < nblk) & (alive > 0))
        def _():
            nxt = copies(nblk - 2 - n, 1 - slot)
            for cp in nxt:
                cp.start()
            for cp in nxt:
                cp.wait()
        return n + 1, alive
    lax.while_loop(lambda c: (c[0] < nblk) & (c[1] > 0), body, (jnp.int32(0), live()))

    o_ref[...] = jnp.concatenate([acc_ref[h] for h in range(heads)], axis=1)


def _mix_b_sample(q, cache_kT, cache_vT, nkT, nvT, layer, tkc):
    b, heads, t, d = q.shape
    assert t & (t - 1) == 0 and heads * t == LANES and tkc >= LANES
    assert cache_kT.shape[4] % tkc == 0
    per_b = lambda *blk: pl.BlockSpec((None,) + blk, lambda bb: (bb,) + (0,) * len(blk))
    tri = _strict_lower(tkc)
    buf = (2, heads, d, tkc)
    return pl.pallas_call(
        functools.partial(_mix_b_sample_kernel, layer=layer, tkc=tkc),
        grid=(b,),
        in_specs=[per_b(heads, t, d), per_b(heads, d, t), per_b(heads, d, t), pl.BlockSpec((tkc, tkc), lambda bb: (0, 0)),
                  pl.BlockSpec(memory_space=pl.ANY), pl.BlockSpec(memory_space=pl.ANY)],
        out_specs=per_b(t, heads * d),
        out_shape=jax.ShapeDtypeStruct((b, t, heads * d), F32),
        scratch_shapes=[pltpu.VMEM(buf, F32), pltpu.VMEM(buf, F32), pltpu.SemaphoreType.DMA((2, 2)),
                        pltpu.VMEM((heads * t, 1), F32), pltpu.VMEM((heads, t, d), F32),
                        pltpu.VMEM((2, heads, d, LANES), F32)],
        compiler_params=_params("arbitrary"),
        name="mix_b_sample",
    )(q, nkT, nvT, tri, cache_kT, cache_vT)


def _merge_kernel(x_ref, oa_ref, ob_ref, gate_ref, g1_ref, sh2_ref, sc2_ref, n2_ref,
                  wa_ref, wb_ref, wo_ref, wr_ref, br_ref, x1_o, h2_o, rg_o):
    nb, tr, d = x_ref.shape
    tm = nb * tr
    a = _dot(oa_ref[...], wa_ref[...])
    bb = _dot(ob_ref[...], wb_ref[...])
    merged = gate_ref[:, 0:d] * a + gate_ref[:, d:2 * d] * bb
    mix = _dot(merged.astype(BF16), wo_ref[...])
    x1 = x_ref[...] + g1_ref[...] * mix.reshape(nb, tr, d)
    x1_o[...] = x1
    ms = jnp.mean(x1 * x1, axis=-1, keepdims=True)
    h2 = ((x1 * lax.rsqrt(ms + RMS_EPS) * n2_ref[...]) * (1.0 + sc2_ref[...]) + sh2_ref[...]).reshape(tm, d)
    h2_o[...] = h2.astype(BF16)

    logits = _dot3(h2, wr_ref[...])
    e = jnp.exp(logits - jnp.max(logits, axis=1, keepdims=True))
    aff = e / jnp.sum(e, axis=1, keepdims=True)
    biased = aff + br_ref[...]
    col = [biased[:, j:j + 1] for j in range(N_EXPERTS)]
    best_val = None
    for g in range(N_GROUPS):
        c = col[g * EXP_PER_GROUP:(g + 1) * EXP_PER_GROUP]
        score = None
        for u in range(EXP_PER_GROUP):
            for w in range(u + 1, EXP_PER_GROUP):
                pair = c[u] + c[w]
                score = pair if score is None else jnp.maximum(score, pair)
        if best_val is None:
            best_val, best_g = score, jnp.zeros((tm, 1), I32)
        else:
            upd = score > best_val
            best_g = jnp.where(upd, g, best_g)
            best_val = jnp.where(upd, score, best_val)
    eidx = lax.broadcasted_iota(I32, (tm, N_EXPERTS), 1)
    cand = jnp.where((eidx >> 2) == best_g, biased, -jnp.inf)
    m1 = jnp.max(cand, axis=1, keepdims=True)
    i1 = jnp.min(jnp.where(cand == m1, eidx, N_EXPERTS), axis=1, keepdims=True)
    cand2 = jnp.where(eidx == i1, -jnp.inf, cand)
    m2 = jnp.max(cand2, axis=1, keepdims=True)
    i2 = jnp.min(jnp.where(cand2 == m2, eidx, N_EXPERTS), axis=1, keepdims=True)
    w1 = jnp.sum(jnp.where(eidx == i1, aff, 0.0), axis=1, keepdims=True)
    w2 = jnp.sum(jnp.where(eidx == i2, aff, 0.0), axis=1, keepdims=True)
    tot = w1 + w2
    rg_o[...] = jnp.where(eidx == i1, w1 / tot, 0.0) + jnp.where(eidx == i2, w2 / tot, 0.0)


def _merge(x, oa, ob, gates, mod, layer, boff, n2, wa, wb, wo, wr, br, nb, tr):
    b, t, d = x.shape
    tpb = t // tr
    tm = nb * tr
    rows = b * t
    grid = (b // nb) * tpb

    def modspec(which):
        return pl.BlockSpec((None, None, nb, 1, d), lambda i: (layer, which, boff // nb + i // tpb, 0, 0))

    xspec = pl.BlockSpec((nb, tr, d), lambda i: (i // tpb, i % tpb, 0))
    row = lambda w: pl.BlockSpec((tm, w), lambda i: (i, 0))
    lw = lambda r, c: pl.BlockSpec((None, r, c), lambda i: (layer, 0, 0))
    return pl.pallas_call(
        _merge_kernel, grid=(grid,),
        in_specs=[xspec, row(512), row(512), row(2 * d), modspec(2), modspec(3), modspec(4), lw(1, d),
                  lw(512, d), lw(512, d), lw(d, d),
                  pl.BlockSpec((d, N_EXPERTS), lambda i: (0, 0)), pl.BlockSpec((1, N_EXPERTS), lambda i: (0, 0))],
        out_specs=[xspec, row(d), row(N_EXPERTS)],
        out_shape=[jax.ShapeDtypeStruct((b, t, d), F32), jax.ShapeDtypeStruct((rows, d), BF16),
                   jax.ShapeDtypeStruct((rows, N_EXPERTS), F32)],
        compiler_params=_params("arbitrary"), name="merge",
    )(x, oa, ob, gates, mod, mod, mod, n2, wa, wb, wo, wr, br)


def _moe_kernel(x_ref, h_ref, rg_ref, g2_ref, wg_ref, wu_ref, wd_ref, o_ref, acc_ref):
    nb, tr, d = x_ref.shape
    tm = nb * tr
    e = pl.program_id(1)

    @pl.when(e == 0)
    def _():
        acc_ref[...] = jnp.zeros(acc_ref.shape, F32)

    h = h_ref[...]
    gate = _dot(h, wg_ref[...])
    up = _dot(h, wu_ref[...])
    hidden = (gate * jax.nn.sigmoid(gate)) * up
    y = _dot(hidden.astype(BF16), wd_ref[...])
    eidx = lax.broadcasted_iota(I32, (tm, N_EXPERTS), 1)
    w = jnp.sum(jnp.where(eidx == e, rg_ref[...], 0.0), axis=1, keepdims=True)
    acc_ref[...] += w * y

    @pl.when(e == N_EXPERTS - 1)
    def _():
        o_ref[...] = x_ref[...] + g2_ref[...] * acc_ref[...].reshape(nb, tr, d)


def _moe(x1, h2, rg, mod, layer, boff, wg, wu, wd, nb, tr):
    b, t, d = x1.shape
    tpb = t // tr
    tm = nb * tr
    grid = (b // nb) * tpb
    de = wg.shape[-1]
    xspec = pl.BlockSpec((nb, tr, d), lambda i, e: (i // tpb, i % tpb, 0))
    return pl.pallas_call(
        _moe_kernel, grid=(grid, N_EXPERTS),
        in_specs=[xspec, pl.BlockSpec((tm, d), lambda i, e: (i, 0)), pl.BlockSpec((tm, N_EXPERTS), lambda i, e: (i, 0)),
                  pl.BlockSpec((None, None, nb, 1, d), lambda i, e: (layer, 5, boff // nb + i // tpb, 0, 0)),
                  pl.BlockSpec((None, None, d, de), lambda i, e: (layer, e, 0, 0)),
                  pl.BlockSpec((None, None, d, de), lambda i, e: (layer, e, 0, 0)),
                  pl.BlockSpec((None, None, de, d), lambda i, e: (layer, e, 0, 0))],
        out_specs=xspec,
        out_shape=jax.ShapeDtypeStruct((b, t, d), F32),
        scratch_shapes=[pltpu.VMEM((tm, d), F32)],
        compiler_params=_params("arbitrary", "arbitrary"), name="moe",
    )(x1, h2, rg, mod, wg, wu, wd)


def _rope_tables(pos):
    inv_freq = ROPE_THETA ** (-jnp.arange(0, HEAD_DIM, 2, dtype=F32) / HEAD_DIM)
    ang = pos.astype(F32)[:, None] * inv_freq[None, :]
    c, s = jnp.cos(ang), jnp.sin(ang)
    cos = jnp.concatenate([c, c, c, c], axis=1)
    sin = jnp.concatenate([-s, s, -s, s], axis=1)
    return cos, sin


def _widen_w_in(w_in):
    sizes = (512, 128, 128, 256, 64, 4, 512, 512, 512, 2048)
    pts = [int(v) for v in np.cumsum(sizes)[:-1]]
    qa, ka, va, qi, ki, wi, qb, kb, vb, gl = jnp.split(w_in, pts, axis=-1)
    dup = lambda w: jnp.concatenate([w[..., 0:64], w[..., 0:64], w[..., 64:128], w[..., 64:128]], axis=-1)
    wi_pad = jnp.pad(wi, ((0, 0), (0, 0), (0, LANES - wi.shape[-1])))
    ext = jnp.concatenate([qa, dup(ka), dup(va), qi, ki, ki, wi_pad, qb, kb, vb, gl], axis=-1)
    assert ext.shape[-1] == N_EXT
    return ext.astype(BF16)


def kernel(x_prompt, x_sample, cache_a_k, cache_a_v, cache_idx_k, cache_b_k, cache_b_v, c_prompt, c_sample,
           norm1_g, norm2_g, w_ada, b_ada, w_in, qn_g, kn_g, w_branch_a, w_branch_b, w_out, w_router, b_router,
           w_exp_gate, w_exp_up, w_exp_down):
    depth = w_in.shape[0]
    bp, tp, d = x_prompt.shape
    bs, ts, _ = x_sample.shape
    past = cache_a_k.shape[2]

    nc = -(-(bs + bp) // 8) * 8
    c_all = jnp.concatenate([c_sample, c_prompt, jnp.zeros((nc - bs - bp, d), F32)], axis=0)
    mod = _ada(c_all, w_ada, b_ada).reshape(depth, 6, nc, 1, d)
    boff_s, boff_p = 0, bs

    w_ext = _widen_w_in(w_in)
    wa, wb, wo = w_branch_a.astype(BF16), w_branch_b.astype(BF16), w_out.astype(BF16)
    wg, wu, wd = w_exp_gate.astype(BF16), w_exp_up.astype(BF16), w_exp_down.astype(BF16)
    n1 = norm1_g.reshape(depth, 1, d)
    n2 = norm2_g.reshape(depth, 1, d)
    qn = jnp.tile(qn_g, (1, 2)).reshape(depth, 1, LANES)
    kn = jnp.tile(kn_g, (1, 2)).reshape(depth, 1, LANES)
    seg = jnp.asarray(np.kron(np.eye(2), np.ones((64, 64))).astype(np.float32), dtype=BF16)
    br = b_router.reshape(1, N_EXPERTS)

    cos_p, sin_p = _rope_tables(jnp.arange(tp, dtype=I32))
    cos_s, sin_s = _rope_tables(past + jnp.arange(ts, dtype=I32))
    cos_s, sin_s = jnp.tile(cos_s, (bs, 1)), jnp.tile(sin_s, (bs, 1))

    tr_p = min(256, tp)
    tq_a = min(256, tp)
    tk_a = min(512, tp)
    tq_b = min(256, tp)
    tm_moe = min(1024, tp)
    tk_s = min(512, past)

    cache_b_kT = jnp.transpose(cache_b_k, (0, 1, 3, 4, 2))
    cache_b_vT = jnp.transpose(cache_b_v, (0, 1, 3, 4, 2))
    cache_a_kT = jnp.transpose(cache_a_k, (0, 1, 3, 4, 2))
    cache_a_vT = jnp.transpose(cache_a_v, (0, 1, 3, 4, 2))
    cache_idx_kT = jnp.transpose(cache_idx_k, (0, 1, 3, 2))

    xp, xs = x_prompt, x_sample
    rows_p, rows_s = [], []
    for l in range(depth):
        p = _in_proj(xp, mod, l, boff_p, n1, w_ext, cos_p, sin_p, qn, kn, seg, 1, tr_p, True)
        oa = _mix_a_prompt(p, bp, tp, tq_a, tk_a)
        ob = _mix_b_prompt(p, bp, tp, tq_b)
        x1, h2, rg = _merge(xp, oa, ob, p["gates"], mod, l, boff_p, n2, wa, wb, wo, w_router, br, 1, tr_p)
        xp = _moe(x1, h2, rg, mod, l, boff_p, wg, wu, wd, 1, tm_moe)
        rows_p.append(p)

        s = _in_proj(xs, mod, l, boff_s, n1, w_ext, cos_s, sin_s, qn, kn, seg, bs, ts, False)
        qi_hm = s["qi"].reshape(bs, ts, IDX_HEADS, IDX_DIM).transpose(0, 2, 1, 3).reshape(bs, IDX_HEADS * ts, IDX_DIM)
        qa5 = s["qa"].reshape(bs, ts, A_KV_HEADS, A_REP, HEAD_DIM).transpose(0, 2, 3, 1, 4)
        qa5 = qa5.reshape(bs, A_KV_HEADS, A_REP * ts, HEAD_DIM)
        kv_major = lambda a: a.reshape(bs, ts, A_KV_HEADS, HEAD_DIM).transpose(0, 2, 3, 1)
        oa_s = _mix_a_sample(qi_hm, s["wi"].reshape(bs, ts, WI_PAD), cache_idx_kT,
                             s["nik"].reshape(bs, ts, IDX_DIM).transpose(0, 2, 1), qa5, cache_a_kT, cache_a_vT,
                             kv_major(s["nak"]), kv_major(s["nav"]), l, tk_s)
        oa_s = oa_s.reshape(bs, A_KV_HEADS, A_REP, ts, HEAD_DIM).transpose(0, 3, 1, 2, 4).reshape(bs * ts, 512)
        heads_first = lambda a: a.reshape(bs, ts, B_HEADS, HEAD_DIM).transpose(0, 2, 1, 3)
        feature_major = lambda a: a.reshape(bs, ts, B_HEADS, HEAD_DIM).transpose(0, 2, 3, 1)
        ob_s = _mix_b_sample(heads_first(s["qb"]), cache_b_kT, cache_b_vT, feature_major(s["nbk"]),
                             feature_major(s["nbv"]), l, min(256, past))
        x1s, h2s, rgs = _merge(xs, oa_s.astype(BF16), ob_s.reshape(bs * ts, 512).astype(BF16), s["gates"], mod, l,
                               boff_s, n2, wa, wb, wo, w_router, br, bs, ts)
        xs = _moe(x1s, h2s, rgs, mod, l, boff_s, wg, wu, wd, bs, ts)
        rows_s.append(s)

    def stack(rows, name, b, t, tail):
        return jnp.stack([r[name].reshape((b, t) + tail) for r in rows], axis=0)

    def stack_t(rows, name, tail):
        a = jnp.stack([r[name] for r in rows], axis=0).reshape((depth, bp) + tail + (tp,))
        return jnp.moveaxis(a, -1, 2)

    kv = (A_KV_HEADS, HEAD_DIM)
    bh = (B_HEADS, HEAD_DIM)
    return (xp, xs,
            stack_t(rows_p, "nak", kv), stack_t(rows_p, "nav", kv), stack_t(rows_p, "nik", (IDX_DIM,)),
            stack_t(rows_p, "nbk", bh), stack_t(rows_p, "nbv", bh),
            stack(rows_s, "nak", bs, ts, kv), stack(rows_s, "nav", bs, ts, kv), stack(rows_s, "nik", bs, ts, (IDX_DIM,)),
            stack(rows_s, "nbk", bs, ts, bh), stack(rows_s, "nbv", bs, ts, bh))
```

```python
import functools

import jax
import jax.numpy as jnp
import numpy as np
from jax import lax
from jax.experimental import pallas as pl
from jax.experimental.pallas import tpu as pltpu

F32 = jnp.float32
BF16 = jnp.bfloat16
I32 = jnp.int32

HEAD_DIM = 64
A_HEADS = 8
A_KV_HEADS = 2
A_REP = A_HEADS // A_KV_HEADS
IDX_HEADS = 4
IDX_DIM = 64
B_HEADS = 8
CHUNK = 64
TOPK_MAX = 256
ROPE_THETA = 10000.0
N_EXPERTS = 16
N_GROUPS = 4
EXP_PER_GROUP = N_EXPERTS // N_GROUPS
RMS_EPS = 1e-6
LANES = 128
LOG2_E = 1.4426950408889634
A_Q_SCALE = HEAD_DIM ** -0.5 * LOG2_E
NEG_BIG = -1e30
MASKED = -3e30
STICK_CUTOFF = -120.0
INT_MIN = -2147483648
KEY_NEG_INF = -2139095041
MIN_NORMAL_BITS = 0x00800000
MIN_NORMAL = 2.0 ** -126
FLOAT_LOWEST = -3.4028234663852886e38
SEARCH_STEPS_PER_TEST = 4
ROW_STRIP = 128

C_QA = 0
C_KAD = 512
C_VAD = 768
C_QI = 1024
C_KID = 1280
C_WI = 1408
C_QB = 1536
C_KB = 2048
C_VB = 2560
C_GL = 3072
N_EXT = 5120
WI_PAD = 8

VMEM_LIMIT = 56 * 1024 * 1024


def _dot(a, b):
    return jnp.dot(a, b, preferred_element_type=F32)


def _dot_nt(a, b):
    return lax.dot_general(a, b, (((1,), (1,)), ((), ())), preferred_element_type=F32)


def _split(a):
    hi = a.astype(BF16)
    lo = (a - hi.astype(F32)).astype(BF16)
    return hi, lo


def _dot3(a, b):
    ah, al = _split(a)
    bh, bl = _split(b)
    return _dot(ah, bh) + _dot(al, bh) + _dot(ah, bl)


def _params(*sem):
    return pltpu.CompilerParams(dimension_semantics=sem, vmem_limit_bytes=VMEM_LIMIT)


def _ada_kernel(c_ref, w_ref, b_ref, o_ref):
    c = c_ref[...]
    s = c * jax.nn.sigmoid(c)
    o_ref[...] = _dot3(s, w_ref[...]) + b_ref[...]


def _ada(c_all, w_ada, b_ada):
    depth, d, _ = w_ada.shape
    nc = c_all.shape[0]
    return pl.pallas_call(
        _ada_kernel,
        grid=(depth, 6),
        in_specs=[
            pl.BlockSpec((nc, d), lambda l, j: (0, 0)),
            pl.BlockSpec((None, d, d), lambda l, j: (l, 0, j)),
            pl.BlockSpec((None, 1, d), lambda l, j: (l, 0, j)),
        ],
        out_specs=pl.BlockSpec((None, None, nc, d), lambda l, j: (l, j, 0, 0)),
        out_shape=jax.ShapeDtypeStruct((depth, 6, nc, d), F32),
        compiler_params=_params("arbitrary", "arbitrary"),
        name="ada_mod",
    )(c_all, w_ada, b_ada.reshape(depth, 1, 6 * d))


def _rope(v, cos, sin, first_half):
    partner = jnp.where(first_half, pltpu.roll(v, LANES - 32, 1), pltpu.roll(v, 32, 1))
    return v * cos + partner * sin


_IN_PROJ_ROWS = (("qa", 512, BF16), ("qi", 256, BF16), ("wi", WI_PAD, F32), ("qb", 512, BF16), ("gates", 2048, F32))
_IN_PROJ_NEW_ROWS = (("nak", 128), ("nav", 128), ("nik", 64), ("nbk", 512), ("nbv", 512))
_IN_PROJ_PROMPT_ROWS = (("vad", 256, BF16), ("vbh", 512, BF16))
_IN_PROJ_PROMPT_KEYS = (("kadT", 256), ("kidT", 128), ("kbT", 512))
KEY_BLOCK = 256


def _in_proj_kernel(x_ref, sh_ref, sc_ref, g_ref, w_ref, cos_ref, sin_ref, qn_ref, kn_ref, seg_ref, *out_refs,
                    names):
    o = dict(zip(names, out_refs))
    prompt = "kadT" in o
    nb, tr, d = x_ref.shape
    tm = nb * tr
    x = x_ref[...]
    ms = jnp.mean(x * x, axis=-1, keepdims=True)
    h = (x * lax.rsqrt(ms + RMS_EPS) * g_ref[...]) * (1.0 + sc_ref[...]) + sh_ref[...]
    hb = h.reshape(tm, d).astype(BF16)

    def proj(lo, width):
        return _dot(hb, w_ref[:, lo:lo + width])

    cos = cos_ref[...]
    sin = sin_ref[...]
    seg = seg_ref[...]
    lane = lax.broadcasted_iota(I32, (tm, LANES), 1)
    first_half = (lane & 32) == 0
    low64 = lane < 64

    def head_norm(v, gain):
        v2 = v * v
        v2h, v2l = _split(v2)
        msq = (_dot(v2h, seg) + _dot(v2l, seg)) * (1.0 / HEAD_DIM)
        return v * lax.rsqrt(msq + RMS_EPS) * gain

    qn = qn_ref[...]
    kn = kn_ref[...]
    v = proj(C_QA, 512)
    for c in range(4):
        blk = _rope(head_norm(v[:, c * LANES:(c + 1) * LANES], qn), cos, sin, first_half)
        o["qa"][:, c * LANES:(c + 1) * LANES] = (blk * A_Q_SCALE).astype(BF16)
    v = proj(C_KAD, 256)
    kd = [_rope(head_norm(v[:, c * LANES:(c + 1) * LANES], kn), cos, sin, first_half) for c in range(2)]
    v = proj(C_VAD, 256)
    if prompt:
        for c in range(2):
            kT = kd[c].T
            o["kadT"][c * LANES:(c + 1) * LANES, :] = kT.astype(BF16)
            o["nak"][c * HEAD_DIM:(c + 1) * HEAD_DIM, :] = kT[0:HEAD_DIM, :]
            o["nav"][c * HEAD_DIM:(c + 1) * HEAD_DIM, :] = v[:, c * LANES:(c + 1) * LANES].T[0:HEAD_DIM, :]
        ones_col = jnp.where(lane == 64, 1.0, 0.0)
        for c in range(2):
            blk = jnp.where(low64, v[:, c * LANES:(c + 1) * LANES], ones_col)
            o["vad"][:, c * LANES:(c + 1) * LANES] = blk.astype(BF16)
    else:
        o["nak"][...] = jnp.where(low64, kd[0], kd[1])
        o["nav"][...] = jnp.where(low64, v[:, 0:LANES], v[:, LANES:2 * LANES])
    v = proj(C_QI, 256)
    for c in range(2):
        blk = _rope(v[:, c * LANES:(c + 1) * LANES], cos, sin, first_half)
        o["qi"][:, c * LANES:(c + 1) * LANES] = (blk * (IDX_DIM ** -0.5)).astype(BF16)
    v = _rope(proj(C_KID, LANES), cos, sin, first_half)
    if prompt:
        kT = v.T
        o["kidT"][...] = kT.astype(BF16)
        o["nik"][...] = kT[0:IDX_DIM, :]
    else:
        o["nik"][...] = v[:, 0:IDX_DIM]
    v = proj(C_WI, LANES)
    o["wi"][...] = v[:, 0:WI_PAD] * (IDX_HEADS ** -0.5)
    o["qb"][...] = (proj(C_QB, 512) * (HEAD_DIM ** -0.5)).astype(BF16)
    v = proj(C_KB, 512)
    if prompt:
        kT = v.T
        o["kbT"][...] = kT.astype(BF16)
        o["nbk"][...] = kT
    else:
        o["nbk"][...] = v
    v = proj(C_VB, 512)
    if prompt:
        o["vbh"][...] = v.astype(BF16)
        o["nbv"][...] = v.T
    else:
        o["nbv"][...] = v
    for c in range(4):
        o["gates"][:, c * 512:(c + 1) * 512] = jax.nn.sigmoid(proj(C_GL + c * 512, 512))


def _in_proj(x, mod, layer, boff, norm_g, w_ext, cos, sin, qn, kn, seg, nb, tr, prompt):
    b, t, d = x.shape
    tpb = t // tr
    tm = nb * tr
    rows = b * t
    grid = (b // nb) * tpb

    def modspec(which):
        return pl.BlockSpec((None, None, nb, 1, d), lambda i: (layer, which, boff // nb + i // tpb, 0, 0))

    in_specs = [
        pl.BlockSpec((nb, tr, d), lambda i: (i // tpb, i % tpb, 0)),
        modspec(0), modspec(1),
        pl.BlockSpec((None, 1, d), lambda i: (layer, 0, 0)),
        pl.BlockSpec((None, d, N_EXT), lambda i: (layer, 0, 0)),
        pl.BlockSpec((tm, LANES), lambda i: (i % tpb, 0)),
        pl.BlockSpec((tm, LANES), lambda i: (i % tpb, 0)),
        pl.BlockSpec((None, 1, LANES), lambda i: (layer, 0, 0)),
        pl.BlockSpec((None, 1, LANES), lambda i: (layer, 0, 0)),
        pl.BlockSpec((LANES, LANES), lambda i: (0, 0)),
    ]
    row_outs = _IN_PROJ_ROWS + (_IN_PROJ_PROMPT_ROWS if prompt else tuple((n, w, F32) for n, w in _IN_PROJ_NEW_ROWS))
    names = [n for n, _, _ in row_outs]
    out_specs = [pl.BlockSpec((tm, w), lambda i: (i, 0)) for _, w, _ in row_outs]
    out_shape = [jax.ShapeDtypeStruct((rows, w), dt) for _, w, dt in row_outs]
    if prompt:
        assert nb == 1 and tr == KEY_BLOCK
        for n, w in _IN_PROJ_PROMPT_KEYS:
            names.append(n)
            out_specs.append(pl.BlockSpec((None, None, w, KEY_BLOCK), lambda i: (i // tpb, i % tpb, 0, 0)))
            out_shape.append(jax.ShapeDtypeStruct((b, tpb, w, KEY_BLOCK), BF16))
        for n, w in _IN_PROJ_NEW_ROWS:
            names.append(n)
            out_specs.append(pl.BlockSpec((None, w, KEY_BLOCK), lambda i: (i // tpb, 0, i % tpb)))
            out_shape.append(jax.ShapeDtypeStruct((b, w, t), F32))
    outs = pl.pallas_call(
        functools.partial(_in_proj_kernel, names=tuple(names)),
        grid=(grid,), in_specs=in_specs, out_specs=out_specs, out_shape=out_shape,
        compiler_params=_params("arbitrary"), name="in_proj",
    )(x, mod, mod, norm_g, w_ext, cos, sin, qn, kn, seg)
    return dict(zip(names, outs))


def _candidate_float(key):
    bits = key ^ ((key >> 31) & 0x7FFFFFFF)
    bits = jnp.where((key > 0) & (key < MIN_NORMAL_BITS), MIN_NORMAL_BITS, bits)
    return lax.bitcast_convert_type(bits, F32)


def _lane_fold(m, op=jnp.add):
    acc = m[:, 0:LANES]
    for c in range(1, m.shape[1] // LANES):
        acc = op(acc, m[:, c * LANES:(c + 1) * LANES])
    return acc


def _select_top(sc_ref, cnt_ref, nkb, rows, tk, n_sel):
    def strip_hits(kb, r0, strip, compare, c_b):
        hits = None
        for j in range(tk // LANES):
            h = jnp.where(compare(sc_ref[kb, r0:r0 + strip, j * LANES:(j + 1) * LANES], c_b), 1.0, 0.0)
            hits = h if hits is None else hits + h
        return hits

    def count(compare, c):
        if rows < ROW_STRIP:
            c_b = jnp.broadcast_to(c, (rows, LANES))
            acc = lax.fori_loop(0, nkb, lambda kb, acc: acc + strip_hits(kb, 0, rows, compare, c_b),
                                jnp.zeros((rows, LANES), F32))
            return jnp.sum(acc, axis=1, keepdims=True)
        strip = ROW_STRIP
        c_b = [jnp.broadcast_to(c[r0:r0 + strip, :], (strip, LANES)) for r0 in range(0, rows, strip)]
        cnt_ref[...] = jnp.zeros(cnt_ref.shape, F32)

        def body(kb, _):
            for i, r0 in enumerate(range(0, rows, strip)):
                cnt_ref[r0:r0 + strip, :] += strip_hits(kb, r0, strip, compare, c_b[i])
            return 0
        lax.fori_loop(0, nkb, body, 0)
        return jnp.sum(cnt_ref[...], axis=1, keepdims=True)

    def step(it, v, hit):
        cand = v ^ jnp.left_shift(jnp.int32(1), 31 - it)
        cnt = count(jnp.greater_equal, _candidate_float(cand))
        ok = cnt >= n_sel
        return jnp.where(ok, cand, v), hit | jnp.where(ok & (cnt == n_sel), 1, 0)

    def steps(c):
        g, v, hit, _ = c
        for u in range(SEARCH_STEPS_PER_TEST):
            v, hit = step(g * SEARCH_STEPS_PER_TEST + u, v, hit)
        return g + 1, v, hit, jnp.min(hit)
    init = (jnp.int32(0), jnp.full((rows, 1), INT_MIN, I32), jnp.zeros((rows, 1), I32), jnp.int32(0))
    _, v, hit, all_hit = lax.while_loop(lambda c: (c[0] < 32 // SEARCH_STEPS_PER_TEST) & (c[3] == 0), steps, init)
    thr = jnp.where(v < KEY_NEG_INF, -jnp.inf, _candidate_float(v))

    @pl.when(all_hit == 0)
    def _():
        cnt_ge = count(jnp.greater_equal, thr)
        need = n_sel - count(jnp.greater, thr)
        tied = (thr > -jnp.inf) & (cnt_ge > n_sel)

        @pl.when(jnp.max(jnp.where(tied, 1, 0)) > 0)
        def _():
            r_i = lax.broadcasted_iota(I32, (tk, tk), 0)
            c_i = lax.broadcasted_iota(I32, (tk, tk), 1)
            upper = jnp.where(r_i <= c_i, 1.0, 0.0).astype(BF16)

            def body(kb, carry):
                s = sc_ref[kb]
                eq = s == thr
                prefix = _dot(jnp.where(eq, 1.0, 0.0).astype(BF16), upper) + carry
                sc_ref[kb] = jnp.where(eq & (prefix > need), -jnp.inf, s)
                return prefix[:, tk - 1:tk]
            lax.fori_loop(0, nkb, body, jnp.zeros((rows, 1), F32))

    return jnp.maximum(thr, FLOAT_LOWEST)


def _mix_a_prompt_kernel(qi_ref, wi_ref, kidT_ref, qa_ref, kadT_ref, vad_ref, o_ref,
                         sc_ref, cnt_ref, bias_ref, bias2_ref, qm_ref, lg_ref, lg2_ref, p_ref, p2_ref, m_ref, mn_ref,
                         acc_ref, *, tk, n_sel):
    tq = qi_ref.shape[0]
    nsb = tk // KEY_BLOCK
    i = pl.program_id(1)
    nkb = ((i + 1) * tq + tk - 1) // tk
    lane = lax.broadcasted_iota(I32, (1, LANES), 1)
    half_mask = [jnp.where(lane < 64, 1.0, 0.0).astype(BF16), jnp.where(lane >= 64, 1.0, 0.0).astype(BF16)]

    def stack_heads(src_ref, dst, first):
        for r in range(A_REP):
            h = first + r
            dst[r * tq:(r + 1) * tq, :] = src_ref[:, (h // 2) * LANES:(h // 2 + 1) * LANES] * half_mask[h % 2]

    wi = wi_ref[...]
    stack_heads(qi_ref, qm_ref.at[0], 0)
    q_chunk = (i * tq + lax.broadcasted_iota(I32, (tq, 1), 0)) >> 6

    def score_block(kb, _):
        for sb in range(nsb):
            dots = _dot(qm_ref[0], kidT_ref[kb * nsb + sb])
            s = jnp.zeros((tq, KEY_BLOCK), F32)
            for h in range(IDX_HEADS):
                s = s + jnp.maximum(dots[h * tq:(h + 1) * tq, :], 0.0) * wi[:, h:h + 1]
            k_chunk = (kb * tk + sb * KEY_BLOCK + lax.broadcasted_iota(I32, (1, KEY_BLOCK), 1)) >> 6
            s = jnp.where(k_chunk <= q_chunk, jnp.where(jnp.abs(s) < MIN_NORMAL, 0.0, s), -jnp.inf)
            sc_ref[kb, :, sb * KEY_BLOCK:(sb + 1) * KEY_BLOCK] = s
        return 0
    lax.fori_loop(0, nkb, score_block, 0)

    thr = _select_top(sc_ref, cnt_ref, nkb, tq, tk, n_sel)

    m_ref[...] = jnp.full(m_ref.shape, NEG_BIG, F32)
    acc_ref[...] = jnp.zeros(acc_ref.shape, F32)
    for g in range(A_KV_HEADS):
        stack_heads(qa_ref, qm_ref.at[g], g * A_REP)

    def masked_logits(kb, lg, bias, g):
        if g == 0:
            bias[...] = jnp.where(sc_ref[kb] >= thr, 0.0, MASKED)
        for sb in range(nsb):
            cols = slice(sb * KEY_BLOCK, (sb + 1) * KEY_BLOCK)
            logits = _dot(qm_ref[g], kadT_ref[kb * nsb + sb, g * LANES:(g + 1) * LANES, :])
            for r in range(A_REP):
                lg[g, r * tq:(r + 1) * tq, cols] = logits[r * tq:(r + 1) * tq, :] + bias[:, cols]

    def accumulate(kb, lg, p, g):
        start = pl.multiple_of(kb * tk, tk)
        for r0 in range(0, A_REP * tq, ROW_STRIP):
            rows = slice(r0, r0 + ROW_STRIP)
            logits = lg[g, rows, :]
            m_new = jnp.maximum(m_ref[g, rows, :], jnp.max(_lane_fold(logits, jnp.maximum), axis=1, keepdims=True))
            mn_ref[g, rows, :] = m_new
            p[g, rows, :] = jnp.exp2(logits - m_new).astype(BF16)
        vblk = vad_ref[pl.ds(start, tk), g * LANES:(g + 1) * LANES]
        pv = _dot(p[g], vblk)
        for r0 in range(0, A_REP * tq, ROW_STRIP):
            rows = slice(r0, r0 + ROW_STRIP)
            alpha = jnp.exp2(m_ref[g, rows, :] - mn_ref[g, rows, :])
            acc_ref[g, rows, :] = alpha * acc_ref[g, rows, :] + pv[r0:r0 + ROW_STRIP, :]
            m_ref[g, rows, :] = mn_ref[g, rows, :]

    for g in range(A_KV_HEADS):
        masked_logits(0, lg_ref, bias_ref, g)

    def attn_pair(j, _):
        kb = 2 * j
        for g in range(A_KV_HEADS):
            masked_logits(kb + 1, lg2_ref, bias2_ref, g)
            accumulate(kb, lg_ref, p_ref, g)
        for g in range(A_KV_HEADS):
            masked_logits(jnp.minimum(kb + 2, nkb - 1), lg_ref, bias_ref, g)
            accumulate(kb + 1, lg2_ref, p2_ref, g)
        return 0
    lax.fori_loop(0, nkb // 2, attn_pair, 0)

    @pl.when(nkb % 2 == 1)
    def _():
        for g in range(A_KV_HEADS):
            accumulate(nkb - 1, lg_ref, p_ref, g)

    low64 = lax.broadcasted_iota(I32, (tq, LANES), 1) < 64
    for j in range(A_HEADS // 2):
        g, r = (2 * j) // A_REP, (2 * j) % A_REP
        a0 = acc_ref[g, r * tq:(r + 1) * tq, :]
        a1 = acc_ref[g, (r + 1) * tq:(r + 2) * tq, :]
        lo = a0 / a0[:, 64:65]
        hi = pltpu.roll(a1, 64, 1) / a1[:, 64:65]
        o_ref[:, j * LANES:(j + 1) * LANES] = jnp.where(low64, lo, hi).astype(BF16)


def _mix_a_prompt(p, b, t, tq, tk):
    nq = t // tq
    n_sel = min(TOPK_MAX, t // 4)
    nkb_max = t // tk
    nsub = t // KEY_BLOCK
    qspec = lambda w: pl.BlockSpec((tq, w), lambda bb, i: (bb * nq + i, 0))
    ktspec = lambda w: pl.BlockSpec((None, nsub, w, KEY_BLOCK), lambda bb, i: (bb, 0, 0, 0))
    return pl.pallas_call(
        functools.partial(_mix_a_prompt_kernel, tk=tk, n_sel=n_sel),
        grid=(b, nq),
        in_specs=[qspec(256), qspec(WI_PAD), ktspec(128), qspec(512), ktspec(256),
                  pl.BlockSpec((t, 256), lambda bb, i: (bb, 0))],
        out_specs=qspec(512),
        out_shape=jax.ShapeDtypeStruct((b * t, 512), BF16),
        scratch_shapes=[
            pltpu.VMEM((nkb_max, tq, tk), F32),
            pltpu.VMEM((tq, LANES), F32),
            pltpu.VMEM((tq, tk), F32),
            pltpu.VMEM((tq, tk), F32),
            pltpu.VMEM((A_KV_HEADS, A_REP * tq, LANES), BF16),
            pltpu.VMEM((A_KV_HEADS, A_REP * tq, tk), F32),
            pltpu.VMEM((A_KV_HEADS, A_REP * tq, tk), F32),
            pltpu.VMEM((A_KV_HEADS, A_REP * tq, tk), BF16),
            pltpu.VMEM((A_KV_HEADS, A_REP * tq, tk), BF16),
            pltpu.VMEM((A_KV_HEADS, A_REP * tq, 1), F32),
            pltpu.VMEM((A_KV_HEADS, A_REP * tq, 1), F32),
            pltpu.VMEM((A_KV_HEADS, A_REP * tq, LANES), F32),
        ],
        compiler_params=_params("arbitrary", "arbitrary"),
        name="mix_a_prompt",
    )(p["qi"], p["wi"], p["kidT"], p["qa"], p["kadT"], p["vad"])


def _softplus(z):
    return jnp.maximum(z, 0.0) + jnp.log(1.0 + jnp.exp(-jnp.abs(z)))


def _mix_b_prompt_kernel(q_ref, k_ref, v_ref, tri_ref, o_ref, r_ref, acc_ref):
    tq = q_ref.shape[0]
    i = pl.program_id(2)
    lane = lax.broadcasted_iota(I32, (1, LANES), 1)
    half_mask = [jnp.where(lane < 64, 1.0, 0.0).astype(BF16), jnp.where(lane >= 64, 1.0, 0.0).astype(BF16)]
    q = q_ref[...]
    qs = jnp.concatenate([q * half_mask[0], q * half_mask[1]], axis=0)
    tri = tri_ref[...]
    r_ref[...] = jnp.zeros(r_ref.shape, F32)
    acc_ref[...] = jnp.zeros(acc_ref.shape, F32)

    def block(kb, diag):
        start = pl.multiple_of(kb * tq, tq)
        z = _dot(qs, k_ref[kb])
        sp = _softplus(z)
        log_keep = -sp
        if diag:
            local_q = lax.broadcasted_iota(I32, (2 * tq, tq), 0) & (tq - 1)
            earlier = lax.broadcasted_iota(I32, (2 * tq, tq), 1) < local_q
            log_keep = jnp.where(earlier, log_keep, 0.0)
        lh, ll = _split(log_keep)
        suffix = _dot(jnp.concatenate([lh, ll], axis=0), tri)
        between = suffix[0:2 * tq, :] + suffix[2 * tq:4 * tq, :] + r_ref[...]
        att = jnp.exp(z - sp + between)
        if diag:
            att = jnp.where(earlier, att, 0.0)
        acc_ref[...] += _dot(att.astype(BF16), v_ref[pl.ds(start, tq), :])
        r_ref[...] += jnp.sum(log_keep, axis=1, keepdims=True)

    def live():
        return jnp.where(jnp.max(r_ref[...]) >= STICK_CUTOFF, 1, 0)

    block(i, True)

    def body(c):
        block(i - 1 - c[0], False)
        return c[0] + 1, live()
    lax.while_loop(lambda c: (c[0] < i) & (c[1] > 0), body, (jnp.int32(0), live()))

    low64 = lax.broadcasted_iota(I32, (tq, LANES), 1) < 64
    o_ref[...] = jnp.where(low64, acc_ref[0:tq, :], acc_ref[tq:2 * tq, :]).astype(BF16)


def _strict_lower(n):
    j = np.arange(n)[:, None]
    s = np.arange(n)[None, :]
    return jnp.asarray((j > s).astype(np.float32), dtype=BF16)


def _mix_b_prompt(p, b, t, tq):
    assert tq == KEY_BLOCK
    nq = t // tq
    npair = B_HEADS // 2
    return pl.pallas_call(
        _mix_b_prompt_kernel,
        grid=(b, npair, nq),
        in_specs=[
            pl.BlockSpec((tq, LANES), lambda bb, j, i: (bb * nq + i, j)),
            pl.BlockSpec((None, nq, LANES, KEY_BLOCK), lambda bb, j, i: (bb, 0, j, 0)),
            pl.BlockSpec((t, LANES), lambda bb, j, i: (bb, j)),
            pl.BlockSpec((tq, tq), lambda bb, j, i: (0, 0)),
        ],
        out_specs=pl.BlockSpec((tq, LANES), lambda bb, j, i: (bb * nq + i, j)),
        out_shape=jax.ShapeDtypeStruct((b * t, B_HEADS * HEAD_DIM), BF16),
        scratch_shapes=[pltpu.VMEM((2 * tq, 1), F32), pltpu.VMEM((2 * tq, LANES), F32)],
        compiler_params=_params("arbitrary", "arbitrary", "arbitrary"),
        name="mix_b_prompt",
    )(p["qb"], p["kbT"], p["vbh"], _strict_lower(tq))


def _mix_a_sample_kernel(qi_ref, wi_ref, cki_ref, nki_ref, qa_ref, cka_ref, cva_ref, nka_ref, nva_ref, o_ref,
                         sc_ref, cnt_ref, pad_ref, lg_ref, *, tk, n_sel):
    t = wi_ref.shape[0]
    past = cki_ref.shape[1]
    nkb = past // tk

    wi = wi_ref[...]
    qi = qi_ref[...]
    col = lax.broadcasted_iota(I32, (t, tk), 1)

    def scores(kT):
        d = _dot(qi, kT)
        s = jnp.zeros((t, tk), F32)
        for h in range(IDX_HEADS):
            s = s + jnp.maximum(d[h * t:(h + 1) * t, :], 0.0) * wi[:, h:h + 1]
        return jnp.where(jnp.abs(s) < MIN_NORMAL, 0.0, s)

    for kb in range(nkb):
        sc_ref[kb] = scores(cki_ref[:, kb * tk:(kb + 1) * tk].astype(BF16))
    pad_ref[...] = jnp.zeros(pad_ref.shape, F32)
    pad_ref[0, :, 0:t] = nki_ref[...]
    s_new = scores(pad_ref[0].astype(BF16))
    q_chunk = (past + lax.broadcasted_iota(I32, (t, 1), 0)) >> 6
    admissible = (col < t) & (((past + col) >> 6) <= q_chunk)
    sc_ref[nkb] = jnp.where(admissible, s_new, -jnp.inf)

    thr = _select_top(sc_ref, cnt_ref, nkb + 1, t, tk, n_sel)

    thr_rep = jnp.concatenate([thr] * A_REP, axis=0)
    pad_ref[:, :, 0:t] = nka_ref[...]
    knew = pad_ref[...].astype(BF16)
    pad_ref[:, :, 0:t] = nva_ref[...]
    vnew = pad_ref[...].astype(BF16)

    def key_of(kb, g):
        return knew[g] if kb == nkb else cka_ref[g, :, kb * tk:(kb + 1) * tk].astype(BF16)

    def value_of(kb, g):
        return vnew[g] if kb == nkb else cva_ref[g, :, kb * tk:(kb + 1) * tk].astype(BF16)

    rows = A_REP * t
    running = [jnp.full((rows, LANES), MASKED, F32) for _ in range(A_KV_HEADS)]
    for kb in range(nkb + 1):
        sel = jnp.concatenate([sc_ref[kb]] * A_REP, axis=0) >= thr_rep
        for g in range(A_KV_HEADS):
            logits = jnp.where(sel, _dot(qa_ref[g], key_of(kb, g)), MASKED)
            lg_ref[g, kb] = logits
            running[g] = jnp.maximum(running[g], _lane_fold(logits, jnp.maximum))
    for g in range(A_KV_HEADS):
        m = jnp.max(running[g], axis=1, keepdims=True)
        total = jnp.zeros((rows, LANES), F32)
        acc = jnp.zeros((rows, HEAD_DIM), F32)
        for kb in range(nkb + 1):
            p = jnp.exp2(lg_ref[g, kb] - m)
            total = total + _lane_fold(p)
            acc = acc + _dot_nt(p.astype(BF16), value_of(kb, g))
        o_ref[g] = acc / jnp.sum(total, axis=1, keepdims=True)


def _mix_a_sample(qi_hm, wi, cache_idx_kT, nikT, qa_hm, cache_a_kT, cache_a_vT, nakT, navT, layer, tk):
    b, t, _ = wi.shape
    past = cache_idx_kT.shape[3]
    assert past % tk == 0
    n_sel = min(TOPK_MAX, (past + t) // 4)
    rows = A_REP * t
    per_b = lambda *blk: pl.BlockSpec((None,) + blk, lambda bb: (bb,) + (0,) * len(blk))
    kv_cache = pl.BlockSpec((None, None, A_KV_HEADS, HEAD_DIM, past), lambda bb: (layer, bb, 0, 0, 0))
    return pl.pallas_call(
        functools.partial(_mix_a_sample_kernel, tk=tk, n_sel=n_sel),
        grid=(b,),
        in_specs=[per_b(IDX_HEADS * t, IDX_DIM), per_b(t, WI_PAD),
                  pl.BlockSpec((None, None, IDX_DIM, past), lambda bb: (layer, bb, 0, 0)), per_b(IDX_DIM, t),
                  per_b(A_KV_HEADS, rows, HEAD_DIM), kv_cache, kv_cache,
                  per_b(A_KV_HEADS, HEAD_DIM, t), per_b(A_KV_HEADS, HEAD_DIM, t)],
        out_specs=per_b(A_KV_HEADS, rows, HEAD_DIM),
        out_shape=jax.ShapeDtypeStruct((b, A_KV_HEADS, rows, HEAD_DIM), F32),
        scratch_shapes=[
            pltpu.VMEM((past // tk + 1, t, tk), F32),
            pltpu.VMEM((t, LANES), F32),
            pltpu.VMEM((A_KV_HEADS, HEAD_DIM, tk), F32),
            pltpu.VMEM((A_KV_HEADS, past // tk + 1, rows, tk), F32),
        ],
        compiler_params=_params("arbitrary"),
        name="mix_a_sample",
    )(qi_hm, wi, cache_idx_kT, nikT, qa_hm, cache_a_kT, cache_a_vT, nakT, navT)


def _mix_b_sample_kernel(q_ref, nk_ref, nv_ref, tri_ref, ck_hbm, cv_hbm, o_ref,
                         kbuf, vbuf, sem, r_ref, acc_ref, pad_ref, *, layer, tkc):
    b = pl.program_id(0)
    heads, t, _ = q_ref.shape
    nblk = ck_hbm.shape[4] // tkc
    npad = pad_ref.shape[3]

    def copies(j, slot):
        cols = pl.ds(pl.multiple_of(j * tkc, tkc), tkc)
        return (pltpu.make_async_copy(ck_hbm.at[layer, b, :, :, cols], kbuf.at[slot], sem.at[0, slot]),
                pltpu.make_async_copy(cv_hbm.at[layer, b, :, :, cols], vbuf.at[slot], sem.at[1, slot]))

    def block(k_of, v_of, nk, earlier):
        z = jnp.concatenate([_dot(q_ref[h], k_of(h)) for h in range(heads)], axis=0)
        sp = _softplus(z)
        log_keep = -sp
        if earlier is not None:
            log_keep = jnp.where(earlier, log_keep, 0.0)
        lh, ll = _split(log_keep)
        suffix = _dot(jnp.concatenate([lh, ll], axis=0), tri_ref[0:nk, 0:nk])
        rows = heads * t
        att = jnp.exp(z - sp + suffix[0:rows, :] + suffix[rows:2 * rows, :] + r_ref[...])
        if earlier is not None:
            att = jnp.where(earlier, att, 0.0)
        att = att.astype(BF16)
        for h in range(heads):
            acc_ref[h] += _dot_nt(att[h * t:(h + 1) * t, :], v_of(h))
        r_ref[...] += jnp.sum(log_keep, axis=1, keepdims=True)

    def live():
        return jnp.where(jnp.max(r_ref[...]) >= STICK_CUTOFF, 1, 0)

    for c in copies(nblk - 1, 0):
        c.start()

    r_ref[...] = jnp.zeros(r_ref.shape, F32)
    acc_ref[...] = jnp.zeros(acc_ref.shape, F32)
    pad_ref[...] = jnp.zeros(pad_ref.shape, F32)
    pad_ref[0, :, :, 0:t] = nk_ref[...]
    pad_ref[1, :, :, 0:t] = nv_ref[...]
    key_j = lax.broadcasted_iota(I32, (heads * t, npad), 1)
    query_i = lax.broadcasted_iota(I32, (heads * t, npad), 0) & (t - 1)
    block(lambda h: pad_ref[0, h].astype(BF16), lambda h: pad_ref[1, h].astype(BF16), npad, key_j < query_i)

    for c in copies(nblk - 1, 0):
        c.wait()

    def body(c):
        n = c[0]
        slot = n & 1
        block(lambda h: kbuf[slot, h].astype(BF16), lambda h: vbuf[slot, h].astype(BF16), tkc, None)
        alive = live()

        @pl.when((n + 1 < nblk) & (alive > 0))
        def _():
            nxt = copies(nblk - 2 - n, 1 - slot)
            for cp in nxt:
                cp.start()
            for cp in nxt:
                cp.wait()
        return n + 1, alive
    lax.while_loop(lambda c: (c[0] < nblk) & (c[1] > 0), body, (jnp.int32(0), live()))

    o_ref[...] = jnp.concatenate([acc_ref[h] for h in range(heads)], axis=1)


def _mix_b_sample(q, cache_kT, cache_vT, nkT, nvT, layer, tkc):
    b, heads, t, d = q.shape
    assert t & (t - 1) == 0 and heads * t == LANES and tkc >= LANES
    assert cache_kT.shape[4] % tkc == 0
    per_b = lambda *blk: pl.BlockSpec((None,) + blk, lambda bb: (bb,) + (0,) * len(blk))
    tri = _strict_lower(tkc)
    buf = (2, heads, d, tkc)
    return pl.pallas_call(
        functools.partial(_mix_b_sample_kernel, layer=layer, tkc=tkc),
        grid=(b,),
        in_specs=[per_b(heads, t, d), per_b(heads, d, t), per_b(heads, d, t), pl.BlockSpec((tkc, tkc), lambda bb: (0, 0)),
                  pl.BlockSpec(memory_space=pl.ANY), pl.BlockSpec(memory_space=pl.ANY)],
        out_specs=per_b(t, heads * d),
        out_shape=jax.ShapeDtypeStruct((b, t, heads * d), F32),
        scratch_shapes=[pltpu.VMEM(buf, F32), pltpu.VMEM(buf, F32), pltpu.SemaphoreType.DMA((2, 2)),
                        pltpu.VMEM((heads * t, 1), F32), pltpu.VMEM((heads, t, d), F32),
                        pltpu.VMEM((2, heads, d, LANES), F32)],
        compiler_params=_params("arbitrary"),
        name="mix_b_sample",
    )(q, nkT, nvT, tri, cache_kT, cache_vT)


def _merge_kernel(x_ref, oa_ref, ob_ref, gate_ref, g1_ref, sh2_ref, sc2_ref, n2_ref,
                  wa_ref, wb_ref, wo_ref, wr_ref, br_ref, x1_o, h2_o, rg_o):
    nb, tr, d = x_ref.shape
    tm = nb * tr
    a = _dot(oa_ref[...], wa_ref[...])
    bb = _dot(ob_ref[...], wb_ref[...])
    merged = gate_ref[:, 0:d] * a + gate_ref[:, d:2 * d] * bb
    mix = _dot(merged.astype(BF16), wo_ref[...])
    x1 = x_ref[...] + g1_ref[...] * mix.reshape(nb, tr, d)
    x1_o[...] = x1
    ms = jnp.mean(x1 * x1, axis=-1, keepdims=True)
    h2 = ((x1 * lax.rsqrt(ms + RMS_EPS) * n2_ref[...]) * (1.0 + sc2_ref[...]) + sh2_ref[...]).reshape(tm, d)
    h2_o[...] = h2.astype(BF16)

    logits = _dot3(h2, wr_ref[...])
    e = jnp.exp(logits - jnp.max(logits, axis=1, keepdims=True))
    aff = e / jnp.sum(e, axis=1, keepdims=True)
    biased = aff + br_ref[...]
    col = [biased[:, j:j + 1] for j in range(N_EXPERTS)]
    best_val = None
    for g in range(N_GROUPS):
        c = col[g * EXP_PER_GROUP:(g + 1) * EXP_PER_GROUP]
        score = None
        for u in range(EXP_PER_GROUP):
            for w in range(u + 1, EXP_PER_GROUP):
                pair = c[u] + c[w]
                score = pair if score is None else jnp.maximum(score, pair)
        if best_val is None:
            best_val, best_g = score, jnp.zeros((tm, 1), I32)
        else:
            upd = score > best_val
            best_g = jnp.where(upd, g, best_g)
            best_val = jnp.where(upd, score, best_val)
    eidx = lax.broadcasted_iota(I32, (tm, N_EXPERTS), 1)
    cand = jnp.where((eidx >> 2) == best_g, biased, -jnp.inf)
    m1 = jnp.max(cand, axis=1, keepdims=True)
    i1 = jnp.min(jnp.where(cand == m1, eidx, N_EXPERTS), axis=1, keepdims=True)
    cand2 = jnp.where(eidx == i1, -jnp.inf, cand)
    m2 = jnp.max(cand2, axis=1, keepdims=True)
    i2 = jnp.min(jnp.where(cand2 == m2, eidx, N_EXPERTS), axis=1, keepdims=True)
    w1 = jnp.sum(jnp.where(eidx == i1, aff, 0.0), axis=1, keepdims=True)
    w2 = jnp.sum(jnp.where(eidx == i2, aff, 0.0), axis=1, keepdims=True)
    tot = w1 + w2
    rg_o[...] = jnp.where(eidx == i1, w1 / tot, 0.0) + jnp.where(eidx == i2, w2 / tot, 0.0)


def _merge(x, oa, ob, gates, mod, layer, boff, n2, wa, wb, wo, wr, br, nb, tr):
    b, t, d = x.shape
    tpb = t // tr
    tm = nb * tr
    rows = b * t
    grid = (b // nb) * tpb

    def modspec(which):
        return pl.BlockSpec((None, None, nb, 1, d), lambda i: (layer, which, boff // nb + i // tpb, 0, 0))

    xspec = pl.BlockSpec((nb, tr, d), lambda i: (i // tpb, i % tpb, 0))
    row = lambda w: pl.BlockSpec((tm, w), lambda i: (i, 0))
    lw = lambda r, c: pl.BlockSpec((None, r, c), lambda i: (layer, 0, 0))
    return pl.pallas_call(
        _merge_kernel, grid=(grid,),
        in_specs=[xspec, row(512), row(512), row(2 * d), modspec(2), modspec(3), modspec(4), lw(1, d),
                  lw(512, d), lw(512, d), lw(d, d),
                  pl.BlockSpec((d, N_EXPERTS), lambda i: (0, 0)), pl.BlockSpec((1, N_EXPERTS), lambda i: (0, 0))],
        out_specs=[xspec, row(d), row(N_EXPERTS)],
        out_shape=[jax.ShapeDtypeStruct((b, t, d), F32), jax.ShapeDtypeStruct((rows, d), BF16),
                   jax.ShapeDtypeStruct((rows, N_EXPERTS), F32)],
        compiler_params=_params("arbitrary"), name="merge",
    )(x, oa, ob, gates, mod, mod, mod, n2, wa, wb, wo, wr, br)


def _moe_kernel(x_ref, h_ref, rg_ref, g2_ref, wg_ref, wu_ref, wd_ref, o_ref, acc_ref):
    nb, tr, d = x_ref.shape
    tm = nb * tr
    e = pl.program_id(1)

    @pl.when(e == 0)
    def _():
        acc_ref[...] = jnp.zeros(acc_ref.shape, F32)

    h = h_ref[...]
    gate = _dot(h, wg_ref[...])
    up = _dot(h, wu_ref[...])
    hidden = (gate * jax.nn.sigmoid(gate)) * up
    y = _dot(hidden.astype(BF16), wd_ref[...])
    eidx = lax.broadcasted_iota(I32, (tm, N_EXPERTS), 1)
    w = jnp.sum(jnp.where(eidx == e, rg_ref[...], 0.0), axis=1, keepdims=True)
    acc_ref[...] += w * y

    @pl.when(e == N_EXPERTS - 1)
    def _():
        o_ref[...] = x_ref[...] + g2_ref[...] * acc_ref[...].reshape(nb, tr, d)


def _moe(x1, h2, rg, mod, layer, boff, wg, wu, wd, nb, tr):
    b, t, d = x1.shape
    tpb = t // tr
    tm = nb * tr
    grid = (b // nb) * tpb
    de = wg.shape[-1]
    xspec = pl.BlockSpec((nb, tr, d), lambda i, e: (i // tpb, i % tpb, 0))
    return pl.pallas_call(
        _moe_kernel, grid=(grid, N_EXPERTS),
        in_specs=[xspec, pl.BlockSpec((tm, d), lambda i, e: (i, 0)), pl.BlockSpec((tm, N_EXPERTS), lambda i, e: (i, 0)),
                  pl.BlockSpec((None, None, nb, 1, d), lambda i, e: (layer, 5, boff // nb + i // tpb, 0, 0)),
                  pl.BlockSpec((None, None, d, de), lambda i, e: (layer, e, 0, 0)),
                  pl.BlockSpec((None, None, d, de), lambda i, e: (layer, e, 0, 0)),
                  pl.BlockSpec((None, None, de, d), lambda i, e: (layer, e, 0, 0))],
        out_specs=xspec,
        out_shape=jax.ShapeDtypeStruct((b, t, d), F32),
        scratch_shapes=[pltpu.VMEM((tm, d), F32)],
        compiler_params=_params("arbitrary", "arbitrary"), name="moe",
    )(x1, h2, rg, mod, wg, wu, wd)


def _rope_tables(pos):
    inv_freq = ROPE_THETA ** (-jnp.arange(0, HEAD_DIM, 2, dtype=F32) / HEAD_DIM)
    ang = pos.astype(F32)[:, None] * inv_freq[None, :]
    c, s = jnp.cos(ang), jnp.sin(ang)
    cos = jnp.concatenate([c, c, c, c], axis=1)
    sin = jnp.concatenate([-s, s, -s, s], axis=1)
    return cos, sin


def _widen_w_in(w_in):
    sizes = (512, 128, 128, 256, 64, 4, 512, 512, 512, 2048)
    pts = [int(v) for v in np.cumsum(sizes)[:-1]]
    qa, ka, va, qi, ki, wi, qb, kb, vb, gl = jnp.split(w_in, pts, axis=-1)
    dup = lambda w: jnp.concatenate([w[..., 0:64], w[..., 0:64], w[..., 64:128], w[..., 64:128]], axis=-1)
    wi_pad = jnp.pad(wi, ((0, 0), (0, 0), (0, LANES - wi.shape[-1])))
    ext = jnp.concatenate([qa, dup(ka), dup(va), qi, ki, ki, wi_pad, qb, kb, vb, gl], axis=-1)
    assert ext.shape[-1] == N_EXT
    return ext.astype(BF16)


def kernel(x_prompt, x_sample, cache_a_k, cache_a_v, cache_idx_k, cache_b_k, cache_b_v, c_prompt, c_sample,
           norm1_g, norm2_g, w_ada, b_ada, w_in, qn_g, kn_g, w_branch_a, w_branch_b, w_out, w_router, b_router,
           w_exp_gate, w_exp_up, w_exp_down):
    depth = w_in.shape[0]
    bp, tp, d = x_prompt.shape
    bs, ts, _ = x_sample.shape
    past = cache_a_k.shape[2]

    nc = -(-(bs + bp) // 8) * 8
    c_all = jnp.concatenate([c_sample, c_prompt, jnp.zeros((nc - bs - bp, d), F32)], axis=0)
    mod = _ada(c_all, w_ada, b_ada).reshape(depth, 6, nc, 1, d)
    boff_s, boff_p = 0, bs

    w_ext = _widen_w_in(w_in)
    wa, wb, wo = w_branch_a.astype(BF16), w_branch_b.astype(BF16), w_out.astype(BF16)
    wg, wu, wd = w_exp_gate.astype(BF16), w_exp_up.astype(BF16), w_exp_down.astype(BF16)
    n1 = norm1_g.reshape(depth, 1, d)
    n2 = norm2_g.reshape(depth, 1, d)
    qn = jnp.tile(qn_g, (1, 2)).reshape(depth, 1, LANES)
    kn = jnp.tile(kn_g, (1, 2)).reshape(depth, 1, LANES)
    seg = jnp.asarray(np.kron(np.eye(2), np.ones((64, 64))).astype(np.float32), dtype=BF16)
    br = b_router.reshape(1, N_EXPERTS)

    cos_p, sin_p = _rope_tables(jnp.arange(tp, dtype=I32))
    cos_s, sin_s = _rope_tables(past + jnp.arange(ts, dtype=I32))
    cos_s, sin_s = jnp.tile(cos_s, (bs, 1)), jnp.tile(sin_s, (bs, 1))

    tr_p = min(256, tp)
    tq_a = min(256, tp)
    tk_a = min(512, tp)
    tq_b = min(256, tp)
    tm_moe = min(1024, tp)
    tk_s = min(512, past)

    cache_b_kT = jnp.transpose(cache_b_k, (0, 1, 3, 4, 2))
    cache_b_vT = jnp.transpose(cache_b_v, (0, 1, 3, 4, 2))
    cache_a_kT = jnp.transpose(cache_a_k, (0, 1, 3, 4, 2))
    cache_a_vT = jnp.transpose(cache_a_v, (0, 1, 3, 4, 2))
    cache_idx_kT = jnp.transpose(cache_idx_k, (0, 1, 3, 2))

    xp, xs = x_prompt, x_sample
    rows_p, rows_s = [], []
    for l in range(depth):
        p = _in_proj(xp, mod, l, boff_p, n1, w_ext, cos_p, sin_p, qn, kn, seg, 1, tr_p, True)
        oa = _mix_a_prompt(p, bp, tp, tq_a, tk_a)
        ob = _mix_b_prompt(p, bp, tp, tq_b)
        x1, h2, rg = _merge(xp, oa, ob, p["gates"], mod, l, boff_p, n2, wa, wb, wo, w_router, br, 1, tr_p)
        xp = _moe(x1, h2, rg, mod, l, boff_p, wg, wu, wd, 1, tm_moe)
        rows_p.append(p)

        s = _in_proj(xs, mod, l, boff_s, n1, w_ext, cos_s, sin_s, qn, kn, seg, bs, ts, False)
        qi_hm = s["qi"].reshape(bs, ts, IDX_HEADS, IDX_DIM).transpose(0, 2, 1, 3).reshape(bs, IDX_HEADS * ts, IDX_DIM)
        qa5 = s["qa"].reshape(bs, ts, A_KV_HEADS, A_REP, HEAD_DIM).transpose(0, 2, 3, 1, 4)
        qa5 = qa5.reshape(bs, A_KV_HEADS, A_REP * ts, HEAD_DIM)
        kv_major = lambda a: a.reshape(bs, ts, A_KV_HEADS, HEAD_DIM).transpose(0, 2, 3, 1)
        oa_s = _mix_a_sample(qi_hm, s["wi"].reshape(bs, ts, WI_PAD), cache_idx_kT,
                             s["nik"].reshape(bs, ts, IDX_DIM).transpose(0, 2, 1), qa5, cache_a_kT, cache_a_vT,
                             kv_major(s["nak"]), kv_major(s["nav"]), l, tk_s)
        oa_s = oa_s.reshape(bs, A_KV_HEADS, A_REP, ts, HEAD_DIM).transpose(0, 3, 1, 2, 4).reshape(bs * ts, 512)
        heads_first = lambda a: a.reshape(bs, ts, B_HEADS, HEAD_DIM).transpose(0, 2, 1, 3)
        feature_major = lambda a: a.reshape(bs, ts, B_HEADS, HEAD_DIM).transpose(0, 2, 3, 1)
        ob_s = _mix_b_sample(heads_first(s["qb"]), cache_b_kT, cache_b_vT, feature_major(s["nbk"]),
                             feature_major(s["nbv"]), l, min(256, past))
        x1s, h2s, rgs = _merge(xs, oa_s.astype(BF16), ob_s.reshape(bs * ts, 512).astype(BF16), s["gates"], mod, l,
                               boff_s, n2, wa, wb, wo, w_router, br, bs, ts)
        xs = _moe(x1s, h2s, rgs, mod, l, boff_s, wg, wu, wd, bs, ts)
        rows_s.append(s)

    def stack(rows, name, b, t, tail):
        return jnp.stack([r[name].reshape((b, t) + tail) for r in rows], axis=0)

    def stack_t(rows, name, tail):
        a = jnp.stack([r[name] for r in rows], axis=0).reshape((depth, bp) + tail + (tp,))
        return jnp.moveaxis(a, -1, 2)

    kv = (A_KV_HEADS, HEAD_DIM)
    bh = (B_HEADS, HEAD_DIM)
    return (xp, xs,
            stack_t(rows_p, "nak", kv), stack_t(rows_p, "nav", kv), stack_t(rows_p, "nik", (IDX_DIM,)),
            stack_t(rows_p, "nbk", bh), stack_t(rows_p, "nbv", bh),
            stack(rows_s, "nak", bs, ts, kv), stack(rows_s, "nav", bs, ts, kv), stack(rows_s, "nik", bs, ts, (IDX_DIM,)),
            stack(rows_s, "nbk", bs, ts, bh), stack(rows_s, "nbv", bs, ts, bh))
```

```python
import functools

import jax
import jax.numpy as jnp
import numpy as np
from jax import lax
from jax.experimental import pallas as pl
from jax.experimental.pallas import tpu as pltpu

F32 = jnp.float32
BF16 = jnp.bfloat16
I32 = jnp.int32

HEAD_DIM = 64
A_HEADS = 8
A_KV_HEADS = 2
A_REP = A_HEADS // A_KV_HEADS
IDX_HEADS = 4
IDX_DIM = 64
B_HEADS = 8
CHUNK = 64
TOPK_MAX = 256
ROPE_THETA = 10000.0
N_EXPERTS = 16
N_GROUPS = 4
EXP_PER_GROUP = N_EXPERTS // N_GROUPS
RMS_EPS = 1e-6
LANES = 128
LOG2_E = 1.4426950408889634
A_Q_SCALE = HEAD_DIM ** -0.5 * LOG2_E
NEG_BIG = -1e30
MASKED = -3e30
STICK_CUTOFF = -120.0
INT_MIN = -2147483648
KEY_NEG_INF = -2139095041
MIN_NORMAL_BITS = 0x00800000
MIN_NORMAL = 2.0 ** -126
FLOAT_LOWEST = -3.4028234663852886e38
SEARCH_STEPS_PER_TEST = 4
ROW_STRIP = 128

C_QA = 0
C_KAD = 512
C_VAD = 768
C_QI = 1024
C_KID = 1280
C_WI = 1408
C_QB = 1536
C_KB = 2048
C_VB = 2560
C_GL = 3072
N_EXT = 5120
WI_PAD = 8

VMEM_LIMIT = 56 * 1024 * 1024


def _dot(a, b):
    return jnp.dot(a, b, preferred_element_type=F32)


def _dot_nt(a, b):
    return lax.dot_general(a, b, (((1,), (1,)), ((), ())), preferred_element_type=F32)


def _split(a):
    hi = a.astype(BF16)
    lo = (a - hi.astype(F32)).astype(BF16)
    return hi, lo


def _dot3(a, b):
    ah, al = _split(a)
    bh, bl = _split(b)
    return _dot(ah, bh) + _dot(al, bh) + _dot(ah, bl)


def _params(*sem):
    return pltpu.CompilerParams(dimension_semantics=sem, vmem_limit_bytes=VMEM_LIMIT)


def _ada_kernel(c_ref, w_ref, b_ref, o_ref):
    c = c_ref[...]
    s = c * jax.nn.sigmoid(c)
    o_ref[...] = _dot3(s, w_ref[...]) + b_ref[...]


def _ada(c_all, w_ada, b_ada):
    depth, d, _ = w_ada.shape
    nc = c_all.shape[0]
    return pl.pallas_call(
        _ada_kernel,
        grid=(depth, 6),
        in_specs=[
            pl.BlockSpec((nc, d), lambda l, j: (0, 0)),
            pl.BlockSpec((None, d, d), lambda l, j: (l, 0, j)),
            pl.BlockSpec((None, 1, d), lambda l, j: (l, 0, j)),
        ],
        out_specs=pl.BlockSpec((None, None, nc, d), lambda l, j: (l, j, 0, 0)),
        out_shape=jax.ShapeDtypeStruct((depth, 6, nc, d), F32),
        compiler_params=_params("arbitrary", "arbitrary"),
        name="ada_mod",
    )(c_all, w_ada, b_ada.reshape(depth, 1, 6 * d))


def _rope(v, cos, sin, first_half):
    partner = jnp.where(first_half, pltpu.roll(v, LANES - 32, 1), pltpu.roll(v, 32, 1))
    return v * cos + partner * sin


_IN_PROJ_ROWS = (("qa", 512, BF16), ("qi", 256, BF16), ("wi", WI_PAD, F32), ("qb", 512, BF16), ("gates", 2048, F32))
_IN_PROJ_NEW_ROWS = (("nak", 128), ("nav", 128), ("nik", 64), ("nbk", 512), ("nbv", 512))
_IN_PROJ_PROMPT_ROWS = (("vad", 256, BF16), ("vbh", 512, BF16))
_IN_PROJ_PROMPT_KEYS = (("kadT", 256), ("kidT", 128), ("kbT", 512))
KEY_BLOCK = 256


def _in_proj_kernel(x_ref, sh_ref, sc_ref, g_ref, w_ref, cos_ref, sin_ref, qn_ref, kn_ref, seg_ref, *out_refs,
                    names):
    o = dict(zip(names, out_refs))
    prompt = "kadT" in o
    nb, tr, d = x_ref.shape
    tm = nb * tr
    x = x_ref[...]
    ms = jnp.mean(x * x, axis=-1, keepdims=True)
    h = (x * lax.rsqrt(ms + RMS_EPS) * g_ref[...]) * (1.0 + sc_ref[...]) + sh_ref[...]
    hb = h.reshape(tm, d).astype(BF16)

    def proj(lo, width):
        return _dot(hb, w_ref[:, lo:lo + width])

    cos = cos_ref[...]
    sin = sin_ref[...]
    seg = seg_ref[...]
    lane = lax.broadcasted_iota(I32, (tm, LANES), 1)
    first_half = (lane & 32) == 0
    low64 = lane < 64

    def head_norm(v, gain):
        v2 = v * v
        v2h, v2l = _split(v2)
        msq = (_dot(v2h, seg) + _dot(v2l, seg)) * (1.0 / HEAD_DIM)
        return v * lax.rsqrt(msq + RMS_EPS) * gain

    qn = qn_ref[...]
    kn = kn_ref[...]
    v = proj(C_QA, 512)
    for c in range(4):
        blk = _rope(head_norm(v[:, c * LANES:(c + 1) * LANES], qn), cos, sin, first_half)
        o["qa"][:, c * LANES:(c + 1) * LANES] = (blk * A_Q_SCALE).astype(BF16)
    v = proj(C_KAD, 256)
    kd = [_rope(head_norm(v[:, c * LANES:(c + 1) * LANES], kn), cos, sin, first_half) for c in range(2)]
    v = proj(C_VAD, 256)
    if prompt:
        for c in range(2):
            kT = kd[c].T
            o["kadT"][c * LANES:(c + 1) * LANES, :] = kT.astype(BF16)
            o["nak"][c * HEAD_DIM:(c + 1) * HEAD_DIM, :] = kT[0:HEAD_DIM, :]
            o["nav"][c * HEAD_DIM:(c + 1) * HEAD_DIM, :] = v[:, c * LANES:(c + 1) * LANES].T[0:HEAD_DIM, :]
        ones_col = jnp.where(lane == 64, 1.0, 0.0)
        for c in range(2):
            blk = jnp.where(low64, v[:, c * LANES:(c + 1) * LANES], ones_col)
            o["vad"][:, c * LANES:(c + 1) * LANES] = blk.astype(BF16)
    else:
        o["nak"][...] = jnp.where(low64, kd[0], kd[1])
        o["nav"][...] = jnp.where(low64, v[:, 0:LANES], v[:, LANES:2 * LANES])
    v = proj(C_QI, 256)
    for c in range(2):
        blk = _rope(v[:, c * LANES:(c + 1) * LANES], cos, sin, first_half)
        o["qi"][:, c * LANES:(c + 1) * LANES] = (blk * (IDX_DIM ** -0.5)).astype(BF16)
    v = _rope(proj(C_KID, LANES), cos, sin, first_half)
    if prompt:
        kT = v.T
        o["kidT"][...] = kT.astype(BF16)
        o["nik"][...] = kT[0:IDX_DIM, :]
    else:
        o["nik"][...] = v[:, 0:IDX_DIM]
    v = proj(C_WI, LANES)
    o["wi"][...] = v[:, 0:WI_PAD] * (IDX_HEADS ** -0.5)
    o["qb"][...] = (proj(C_QB, 512) * (HEAD_DIM ** -0.5)).astype(BF16)
    v = proj(C_KB, 512)
    if prompt:
        kT = v.T
        o["kbT"][...] = kT.astype(BF16)
        o["nbk"][...] = kT
    else:
        o["nbk"][...] = v
    v = proj(C_VB, 512)
    if prompt:
        o["vbh"][...] = v.astype(BF16)
        o["nbv"][...] = v.T
    else:
        o["nbv"][...] = v
    for c in range(4):
        o["gates"][:, c * 512:(c + 1) * 512] = jax.nn.sigmoid(proj(C_GL + c * 512, 512))


def _in_proj(x, mod, layer, boff, norm_g, w_ext, cos, sin, qn, kn, seg, nb, tr, prompt):
    b, t, d = x.shape
    tpb = t // tr
    tm = nb * tr
    rows = b * t
    grid = (b // nb) * tpb

    def modspec(which):
        return pl.BlockSpec((None, None, nb, 1, d), lambda i: (layer, which, boff // nb + i // tpb, 0, 0))

    in_specs = [
        pl.BlockSpec((nb, tr, d), lambda i: (i // tpb, i % tpb, 0)),
        modspec(0), modspec(1),
        pl.BlockSpec((None, 1, d), lambda i: (layer, 0, 0)),
        pl.BlockSpec((None, d, N_EXT), lambda i: (layer, 0, 0)),
        pl.BlockSpec((tm, LANES), lambda i: (i % tpb, 0)),
        pl.BlockSpec((tm, LANES), lambda i: (i % tpb, 0)),
        pl.BlockSpec((None, 1, LANES), lambda i: (layer, 0, 0)),
        pl.BlockSpec((None, 1, LANES), lambda i: (layer, 0, 0)),
        pl.BlockSpec((LANES, LANES), lambda i: (0, 0)),
    ]
    row_outs = _IN_PROJ_ROWS + (_IN_PROJ_PROMPT_ROWS if prompt else tuple((n, w, F32) for n, w in _IN_PROJ_NEW_ROWS))
    names = [n for n, _, _ in row_outs]
    out_specs = [pl.BlockSpec((tm, w), lambda i: (i, 0)) for _, w, _ in row_outs]
    out_shape = [jax.ShapeDtypeStruct((rows, w), dt) for _, w, dt in row_outs]
    if prompt:
        assert nb == 1 and tr == KEY_BLOCK
        for n, w in _IN_PROJ_PROMPT_KEYS:
            names.append(n)
            out_specs.append(pl.BlockSpec((None, None, w, KEY_BLOCK), lambda i: (i // tpb, i % tpb, 0, 0)))
            out_shape.append(jax.ShapeDtypeStruct((b, tpb, w, KEY_BLOCK), BF16))
        for n, w in _IN_PROJ_NEW_ROWS:
            names.append(n)
            out_specs.append(pl.BlockSpec((None, w, KEY_BLOCK), lambda i: (i // tpb, 0, i % tpb)))
            out_shape.append(jax.ShapeDtypeStruct((b, w, t), F32))
    outs = pl.pallas_call(
        functools.partial(_in_proj_kernel, names=tuple(names)),
        grid=(grid,), in_specs=in_specs, out_specs=out_specs, out_shape=out_shape,
        compiler_params=_params("arbitrary"), name="in_proj",
    )(x, mod, mod, norm_g, w_ext, cos, sin, qn, kn, seg)
    return dict(zip(names, outs))


def _candidate_float(key):
    bits = key ^ ((key >> 31) & 0x7FFFFFFF)
    bits = jnp.where((key > 0) & (key < MIN_NORMAL_BITS), MIN_NORMAL_BITS, bits)
    return lax.bitcast_convert_type(bits, F32)


def _lane_fold(m, op=jnp.add):
    acc = m[:, 0:LANES]
    for c in range(1, m.shape[1] // LANES):
        acc = op(acc, m[:, c * LANES:(c + 1) * LANES])
    return acc


def _select_top(sc_ref, cnt_ref, nkb, rows, tk, n_sel):
    def strip_hits(kb, r0, strip, compare, c_b):
        hits = None
        for j in range(tk // LANES):
            h = jnp.where(compare(sc_ref[kb, r0:r0 + strip, j * LANES:(j + 1) * LANES], c_b), 1.0, 0.0)
            hits = h if hits is None else hits + h
        return hits

    def count(compare, c):
        if rows < ROW_STRIP:
            c_b = jnp.broadcast_to(c, (rows, LANES))
            acc = lax.fori_loop(0, nkb, lambda kb, acc: acc + strip_hits(kb, 0, rows, compare, c_b),
                                jnp.zeros((rows, LANES), F32))
            return jnp.sum(acc, axis=1, keepdims=True)
        strip = ROW_STRIP
        c_b = [jnp.broadcast_to(c[r0:r0 + strip, :], (strip, LANES)) for r0 in range(0, rows, strip)]
        cnt_ref[...] = jnp.zeros(cnt_ref.shape, F32)

        def body(kb, _):
            for i, r0 in enumerate(range(0, rows, strip)):
                cnt_ref[r0:r0 + strip, :] += strip_hits(kb, r0, strip, compare, c_b[i])
            return 0
        lax.fori_loop(0, nkb, body, 0)
        return jnp.sum(cnt_ref[...], axis=1, keepdims=True)

    def step(it, v, hit):
        cand = v ^ jnp.left_shift(jnp.int32(1), 31 - it)
        cnt = count(jnp.greater_equal, _candidate_float(cand))
        ok = cnt >= n_sel
        return jnp.where(ok, cand, v), hit | jnp.where(ok & (cnt == n_sel), 1, 0)

    def steps(c):
        g, v, hit, _ = c
        for u in range(SEARCH_STEPS_PER_TEST):
            v, hit = step(g * SEARCH_STEPS_PER_TEST + u, v, hit)
        return g + 1, v, hit, jnp.min(hit)
    init = (jnp.int32(0), jnp.full((rows, 1), INT_MIN, I32), jnp.zeros((rows, 1), I32), jnp.int32(0))
    _, v, hit, all_hit = lax.while_loop(lambda c: (c[0] < 32 // SEARCH_STEPS_PER_TEST) & (c[3] == 0), steps, init)
    thr = jnp.where(v < KEY_NEG_INF, -jnp.inf, _candidate_float(v))

    @pl.when(all_hit == 0)
    def _():
        cnt_ge = count(jnp.greater_equal, thr)
        need = n_sel - count(jnp.greater, thr)
        tied = (thr > -jnp.inf) & (cnt_ge > n_sel)

        @pl.when(jnp.max(jnp.where(tied, 1, 0)) > 0)
        def _():
            r_i = lax.broadcasted_iota(I32, (tk, tk), 0)
            c_i = lax.broadcasted_iota(I32, (tk, tk), 1)
            upper = jnp.where(r_i <= c_i, 1.0, 0.0).astype(BF16)

            def body(kb, carry):
                s = sc_ref[kb]
                eq = s == thr
                prefix = _dot(jnp.where(eq, 1.0, 0.0).astype(BF16), upper) + carry
                sc_ref[kb] = jnp.where(eq & (prefix > need), -jnp.inf, s)
                return prefix[:, tk - 1:tk]
            lax.fori_loop(0, nkb, body, jnp.zeros((rows, 1), F32))

    return jnp.maximum(thr, FLOAT_LOWEST)


def _mix_a_prompt_kernel(qi_ref, wi_ref, kidT_ref, qa_ref, kadT_ref, vad_ref, o_ref,
                         sc_ref, cnt_ref, bias_ref, bias2_ref, qm_ref, lg_ref, lg2_ref, p_ref, p2_ref, m_ref, mn_ref,
                         acc_ref, *, tk, n_sel):
    tq = qi_ref.shape[0]
    nsb = tk // KEY_BLOCK
    i = pl.program_id(1)
    nkb = ((i + 1) * tq + tk - 1) // tk
    lane = lax.broadcasted_iota(I32, (1, LANES), 1)
    half_mask = [jnp.where(lane < 64, 1.0, 0.0).astype(BF16), jnp.where(lane >= 64, 1.0, 0.0).astype(BF16)]

    def stack_heads(src_ref, dst, first):
        for r in range(A_REP):
            h = first + r
            dst[r * tq:(r + 1) * tq, :] = src_ref[:, (h // 2) * LANES:(h // 2 + 1) * LANES] * half_mask[h % 2]

    wi = wi_ref[...]
    stack_heads(qi_ref, qm_ref.at[0], 0)
    q_chunk = (i * tq + lax.broadcasted_iota(I32, (tq, 1), 0)) >> 6

    def score_block(kb, _):
        for sb in range(nsb):
            dots = _dot(qm_ref[0], kidT_ref[kb * nsb + sb])
            s = jnp.zeros((tq, KEY_BLOCK), F32)
            for h in range(IDX_HEADS):
                s = s + jnp.maximum(dots[h * tq:(h + 1) * tq, :], 0.0) * wi[:, h:h + 1]
            k_chunk = (kb * tk + sb * KEY_BLOCK + lax.broadcasted_iota(I32, (1, KEY_BLOCK), 1)) >> 6
            s = jnp.where(k_chunk <= q_chunk, jnp.where(jnp.abs(s) < MIN_NORMAL, 0.0, s), -jnp.inf)
            sc_ref[kb, :, sb * KEY_BLOCK:(sb + 1) * KEY_BLOCK] = s
        return 0
    lax.fori_loop(0, nkb, score_block, 0)

    thr = _select_top(sc_ref, cnt_ref, nkb, tq, tk, n_sel)

    m_ref[...] = jnp.full(m_ref.shape, NEG_BIG, F32)
    acc_ref[...] = jnp.zeros(acc_ref.shape, F32)
    for g in range(A_KV_HEADS):
        stack_heads(qa_ref, qm_ref.at[g], g * A_REP)

    def masked_logits(kb, lg, bias, g):
        if g == 0:
            bias[...] = jnp.where(sc_ref[kb] >= thr, 0.0, MASKED)
        for sb in range(nsb):
            cols = slice(sb * KEY_BLOCK, (sb + 1) * KEY_BLOCK)
            logits = _dot(qm_ref[g], kadT_ref[kb * nsb + sb, g * LANES:(g + 1) * LANES, :])
            for r in range(A_REP):
                lg[g, r * tq:(r + 1) * tq, cols] = logits[r * tq:(r + 1) * tq, :] + bias[:, cols]

    def accumulate(kb, lg, p, g):
        start = pl.multiple_of(kb * tk, tk)
        for r0 in range(0, A_REP * tq, ROW_STRIP):
            rows = slice(r0, r0 + ROW_STRIP)
            logits = lg[g, rows, :]
            m_new = jnp.maximum(m_ref[g, rows, :], jnp.max(_lane_fold(logits, jnp.maximum), axis=1, keepdims=True))
            mn_ref[g, rows, :] = m_new
            p[g, rows, :] = jnp.exp2(logits - m_new).astype(BF16)
        vblk = vad_ref[pl.ds(start, tk), g * LANES:(g + 1) * LANES]
        pv = _dot(p[g], vblk)
        for r0 in range(0, A_REP * tq, ROW_STRIP):
            rows = slice(r0, r0 + ROW_STRIP)
            alpha = jnp.exp2(m_ref[g, rows, :] - mn_ref[g, rows, :])
            acc_ref[g, rows, :] = alpha * acc_ref[g, rows, :] + pv[r0:r0 + ROW_STRIP, :]
            m_ref[g, rows, :] = mn_ref[g, rows, :]

    for g in range(A_KV_HEADS):
        masked_logits(0, lg_ref, bias_ref, g)

    def attn_pair(j, _):
        kb = 2 * j
        for g in range(A_KV_HEADS):
            masked_logits(kb + 1, lg2_ref, bias2_ref, g)
            accumulate(kb, lg_ref, p_ref, g)
        for g in range(A_KV_HEADS):
            masked_logits(jnp.minimum(kb + 2, nkb - 1), lg_ref, bias_ref, g)
            accumulate(kb + 1, lg2_ref, p2_ref, g)
        return 0
    lax.fori_loop(0, nkb // 2, attn_pair, 0)

    @pl.when(nkb % 2 == 1)
    def _():
        for g in range(A_KV_HEADS):
            accumulate(nkb - 1, lg_ref, p_ref, g)

    low64 = lax.broadcasted_iota(I32, (tq, LANES), 1) < 64
    for j in range(A_HEADS // 2):
        g, r = (2 * j) // A_REP, (2 * j) % A_REP
        a0 = acc_ref[g, r * tq:(r + 1) * tq, :]
        a1 = acc_ref[g, (r + 1) * tq:(r + 2) * tq, :]
        lo = a0 / a0[:, 64:65]
        hi = pltpu.roll(a1, 64, 1) / a1[:, 64:65]
        o_ref[:, j * LANES:(j + 1) * LANES] = jnp.where(low64, lo, hi).astype(BF16)


def _mix_a_prompt(p, b, t, tq, tk):
    nq = t // tq
    n_sel = min(TOPK_MAX, t // 4)
    nkb_max = t // tk
    nsub = t // KEY_BLOCK
    qspec = lambda w: pl.BlockSpec((tq, w), lambda bb, i: (bb * nq + i, 0))
    ktspec = lambda w: pl.BlockSpec((None, nsub, w, KEY_BLOCK), lambda bb, i: (bb, 0, 0, 0))
    return pl.pallas_call(
        functools.partial(_mix_a_prompt_kernel, tk=tk, n_sel=n_sel),
        grid=(b, nq),
        in_specs=[qspec(256), qspec(WI_PAD), ktspec(128), qspec(512), ktspec(256),
                  pl.BlockSpec((t, 256), lambda bb, i: (bb, 0))],
        out_specs=qspec(512),
        out_shape=jax.ShapeDtypeStruct((b * t, 512), BF16),
        scratch_shapes=[
            pltpu.VMEM((nkb_max, tq, tk), F32),
            pltpu.VMEM((tq, LANES), F32),
            pltpu.VMEM((tq, tk), F32),
            pltpu.VMEM((tq, tk), F32),
            pltpu.VMEM((A_KV_HEADS, A_REP * tq, LANES), BF16),
            pltpu.VMEM((A_KV_HEADS, A_REP * tq, tk), F32),
            pltpu.VMEM((A_KV_HEADS, A_REP * tq, tk), F32),
            pltpu.VMEM((A_KV_HEADS, A_REP * tq, tk), BF16),
            pltpu.VMEM((A_KV_HEADS, A_REP * tq, tk), BF16),
            pltpu.VMEM((A_KV_HEADS, A_REP * tq, 1), F32),
            pltpu.VMEM((A_KV_HEADS, A_REP * tq, 1), F32),
            pltpu.VMEM((A_KV_HEADS, A_REP * tq, LANES), F32),
        ],
        compiler_params=_params("arbitrary", "arbitrary"),
        name="mix_a_prompt",
    )(p["qi"], p["wi"], p["kidT"], p["qa"], p["kadT"], p["vad"])


def _softplus(z):
    return jnp.maximum(z, 0.0) + jnp.log(1.0 + jnp.exp(-jnp.abs(z)))


def _mix_b_prompt_kernel(q_ref, k_ref, v_ref, tri_ref, o_ref, r_ref, acc_ref):
    tq = q_ref.shape[0]
    i = pl.program_id(2)
    lane = lax.broadcasted_iota(I32, (1, LANES), 1)
    half_mask = [jnp.where(lane < 64, 1.0, 0.0).astype(BF16), jnp.where(lane >= 64, 1.0, 0.0).astype(BF16)]
    q = q_ref[...]
    qs = jnp.concatenate([q * half_mask[0], q * half_mask[1]], axis=0)
    tri = tri_ref[...]
    r_ref[...] = jnp.zeros(r_ref.shape, F32)
    acc_ref[...] = jnp.zeros(acc_ref.shape, F32)

    def block(kb, diag):
        start = pl.multiple_of(kb * tq, tq)
        z = _dot(qs, k_ref[kb])
        sp = _softplus(z)
        log_keep = -sp
        if diag:
            local_q = lax.broadcasted_iota(I32, (2 * tq, tq), 0) & (tq - 1)
            earlier = lax.broadcasted_iota(I32, (2 * tq, tq), 1) < local_q
            log_keep = jnp.where(earlier, log_keep, 0.0)
        lh, ll = _split(log_keep)
        suffix = _dot(jnp.concatenate([lh, ll], axis=0), tri)
        between = suffix[0:2 * tq, :] + suffix[2 * tq:4 * tq, :] + r_ref[...]
        att = jnp.exp(z - sp + between)
        if diag:
            att = jnp.where(earlier, att, 0.0)
        acc_ref[...] += _dot(att.astype(BF16), v_ref[pl.ds(start, tq), :])
        r_ref[...] += jnp.sum(log_keep, axis=1, keepdims=True)

    def live():
        return jnp.where(jnp.max(r_ref[...]) >= STICK_CUTOFF, 1, 0)

    @pl.when(i == 0)
    def _():
        block(0, True)

    @pl.when(i > 0)
    def _():
        block(i, True)
        block(i - 1, False)

    def body(c):
        block(i - 1 - c[0], False)
        return c[0] + 1, live()
    lax.while_loop(lambda c: (c[0] < i) & (c[1] > 0), body, (jnp.int32(1), live()))

    low64 = lax.broadcasted_iota(I32, (tq, LANES), 1) < 64
    o_ref[...] = jnp.where(low64, acc_ref[0:tq, :], acc_ref[tq:2 * tq, :]).astype(BF16)


def _strict_lower(n):
    j = np.arange(n)[:, None]
    s = np.arange(n)[None, :]
    return jnp.asarray((j > s).astype(np.float32), dtype=BF16)


def _mix_b_prompt(p, b, t, tq):
    assert tq == KEY_BLOCK
    nq = t // tq
    npair = B_HEADS // 2
    return pl.pallas_call(
        _mix_b_prompt_kernel,
        grid=(b, npair, nq),
        in_specs=[
            pl.BlockSpec((tq, LANES), lambda bb, j, i: (bb * nq + i, j)),
            pl.BlockSpec((None, nq, LANES, KEY_BLOCK), lambda bb, j, i: (bb, 0, j, 0)),
            pl.BlockSpec((t, LANES), lambda bb, j, i: (bb, j)),
            pl.BlockSpec((tq, tq), lambda bb, j, i: (0, 0)),
        ],
        out_specs=pl.BlockSpec((tq, LANES), lambda bb, j, i: (bb * nq + i, j)),
        out_shape=jax.ShapeDtypeStruct((b * t, B_HEADS * HEAD_DIM), BF16),
        scratch_shapes=[pltpu.VMEM((2 * tq, 1), F32), pltpu.VMEM((2 * tq, LANES), F32)],
        compiler_params=_params("arbitrary", "arbitrary", "arbitrary"),
        name="mix_b_prompt",
    )(p["qb"], p["kbT"], p["vbh"], _strict_lower(tq))


def _mix_a_sample_kernel(qi_ref, wi_ref, cki_ref, nki_ref, qa_ref, cka_ref, cva_ref, nka_ref, nva_ref, o_ref,
                         sc_ref, cnt_ref, pad_ref, lg_ref, *, tk, n_sel):
    t = wi_ref.shape[0]
    past = cki_ref.shape[1]
    nkb = past // tk

    wi = wi_ref[...]
    qi = qi_ref[...]
    col = lax.broadcasted_iota(I32, (t, tk), 1)

    def scores(kT):
        d = _dot(qi, kT)
        s = jnp.zeros((t, tk), F32)
        for h in range(IDX_HEADS):
            s = s + jnp.maximum(d[h * t:(h + 1) * t, :], 0.0) * wi[:, h:h + 1]
        return jnp.where(jnp.abs(s) < MIN_NORMAL, 0.0, s)

    for kb in range(nkb):
        sc_ref[kb] = scores(cki_ref[:, kb * tk:(kb + 1) * tk].astype(BF16))
    pad_ref[...] = jnp.zeros(pad_ref.shape, F32)
    pad_ref[0, :, 0:t] = nki_ref[...]
    s_new = scores(pad_ref[0].astype(BF16))
    q_chunk = (past + lax.broadcasted_iota(I32, (t, 1), 0)) >> 6
    admissible = (col < t) & (((past + col) >> 6) <= q_chunk)
    sc_ref[nkb] = jnp.where(admissible, s_new, -jnp.inf)

    thr = _select_top(sc_ref, cnt_ref, nkb + 1, t, tk, n_sel)

    thr_rep = jnp.concatenate([thr] * A_REP, axis=0)
    pad_ref[:, :, 0:t] = nka_ref[...]
    knew = pad_ref[...].astype(BF16)
    pad_ref[:, :, 0:t] = nva_ref[...]
    vnew = pad_ref[...].astype(BF16)

    def key_of(kb, g):
        return knew[g] if kb == nkb else cka_ref[g, :, kb * tk:(kb + 1) * tk].astype(BF16)

    def value_of(kb, g):
        return vnew[g] if kb == nkb else cva_ref[g, :, kb * tk:(kb + 1) * tk].astype(BF16)

    rows = A_REP * t
    running = [jnp.full((rows, LANES), MASKED, F32) for _ in range(A_KV_HEADS)]
    for kb in range(nkb + 1):
        sel = jnp.concatenate([sc_ref[kb]] * A_REP, axis=0) >= thr_rep
        for g in range(A_KV_HEADS):
            logits = jnp.where(sel, _dot(qa_ref[g], key_of(kb, g)), MASKED)
            lg_ref[g, kb] = logits
            running[g] = jnp.maximum(running[g], _lane_fold(logits, jnp.maximum))
    for g in range(A_KV_HEADS):
        m = jnp.max(running[g], axis=1, keepdims=True)
        total = jnp.zeros((rows, LANES), F32)
        acc = jnp.zeros((rows, HEAD_DIM), F32)
        for kb in range(nkb + 1):
            p = jnp.exp2(lg_ref[g, kb] - m)
            total = total + _lane_fold(p)
            acc = acc + _dot_nt(p.astype(BF16), value_of(kb, g))
        o_ref[g] = acc / jnp.sum(total, axis=1, keepdims=True)


def _mix_a_sample(qi_hm, wi, cache_idx_kT, nikT, qa_hm, cache_a_kT, cache_a_vT, nakT, navT, layer, tk):
    b, t, _ = wi.shape
    past = cache_idx_kT.shape[3]
    assert past % tk == 0
    n_sel = min(TOPK_MAX, (past + t) // 4)
    rows = A_REP * t
    per_b = lambda *blk: pl.BlockSpec((None,) + blk, lambda bb: (bb,) + (0,) * len(blk))
    kv_cache = pl.BlockSpec((None, None, A_KV_HEADS, HEAD_DIM, past), lambda bb: (layer, bb, 0, 0, 0))
    return pl.pallas_call(
        functools.partial(_mix_a_sample_kernel, tk=tk, n_sel=n_sel),
        grid=(b,),
        in_specs=[per_b(IDX_HEADS * t, IDX_DIM), per_b(t, WI_PAD),
                  pl.BlockSpec((None, None, IDX_DIM, past), lambda bb: (layer, bb, 0, 0)), per_b(IDX_DIM, t),
                  per_b(A_KV_HEADS, rows, HEAD_DIM), kv_cache, kv_cache,
                  per_b(A_KV_HEADS, HEAD_DIM, t), per_b(A_KV_HEADS, HEAD_DIM, t)],
        out_specs=per_b(A_KV_HEADS, rows, HEAD_DIM),
        out_shape=jax.ShapeDtypeStruct((b, A_KV_HEADS, rows, HEAD_DIM), F32),
        scratch_shapes=[
            pltpu.VMEM((past // tk + 1, t, tk), F32),
            pltpu.VMEM((t, LANES), F32),
            pltpu.VMEM((A_KV_HEADS, HEAD_DIM, tk), F32),
            pltpu.VMEM((A_KV_HEADS, past // tk + 1, rows, tk), F32),
        ],
        compiler_params=_params("arbitrary"),
        name="mix_a_sample",
    )(qi_hm, wi, cache_idx_kT, nikT, qa_hm, cache_a_kT, cache_a_vT, nakT, navT)


def _mix_b_sample_kernel(q_ref, nk_ref, nv_ref, tri_ref, ck_hbm, cv_hbm, o_ref,
                         kbuf, vbuf, sem, r_ref, acc_ref, pad_ref, *, layer, tkc):
    b = pl.program_id(0)
    heads, t, _ = q_ref.shape
    nblk = ck_hbm.shape[4] // tkc
    npad = pad_ref.shape[3]

    def copies(j, slot):
        cols = pl.ds(pl.multiple_of(j * tkc, tkc), tkc)
        return (pltpu.make_async_copy(ck_hbm.at[layer, b, :, :, cols], kbuf.at[slot], sem.at[0, slot]),
                pltpu.make_async_copy(cv_hbm.at[layer, b, :, :, cols], vbuf.at[slot], sem.at[1, slot]))

    def block(k_of, v_of, nk, earlier):
        z = jnp.concatenate([_dot(q_ref[h], k_of(h)) for h in range(heads)], axis=0)
        sp = _softplus(z)
        log_keep = -sp
        if earlier is not None:
            log_keep = jnp.where(earlier, log_keep, 0.0)
        lh, ll = _split(log_keep)
        suffix = _dot(jnp.concatenate([lh, ll], axis=0), tri_ref[0:nk, 0:nk])
        rows = heads * t
        att = jnp.exp(z - sp + suffix[0:rows, :] + suffix[rows:2 * rows, :] + r_ref[...])
        if earlier is not None:
            att = jnp.where(earlier, att, 0.0)
        att = att.astype(BF16)
        for h in range(heads):
            acc_ref[h] += _dot_nt(att[h * t:(h + 1) * t, :], v_of(h))
        r_ref[...] += jnp.sum(log_keep, axis=1, keepdims=True)

    def live():
        return jnp.where(jnp.max(r_ref[...]) >= STICK_CUTOFF, 1, 0)

    for c in copies(nblk - 1, 0):
        c.start()

    r_ref[...] = jnp.zeros(r_ref.shape, F32)
    acc_ref[...] = jnp.zeros(acc_ref.shape, F32)
    pad_ref[...] = jnp.zeros(pad_ref.shape, F32)
    pad_ref[0, :, :, 0:t] = nk_ref[...]
    pad_ref[1, :, :, 0:t] = nv_ref[...]
    key_j = lax.broadcasted_iota(I32, (heads * t, npad), 1)
    query_i = lax.broadcasted_iota(I32, (heads * t, npad), 0) & (t - 1)
    block(lambda h: pad_ref[0, h].astype(BF16), lambda h: pad_ref[1, h].astype(BF16), npad, key_j < query_i)

    for c in copies(nblk - 1, 0):
        c.wait()

    def body(c):
        n = c[0]
        slot = n & 1
        block(lambda h: kbuf[slot, h].astype(BF16), lambda h: vbuf[slot, h].astype(BF16), tkc, None)
        alive = live()

        @pl.when((n + 1 < nblk) & (alive > 0))
        def _():
            nxt = copies(nblk - 2 - n, 1 - slot)
            for cp in nxt:
                cp.start()
            for cp in nxt:
                cp.wait()
        return n + 1, alive
    lax.while_loop(lambda c: (c[0] < nblk) & (c[1] > 0), body, (jnp.int32(0), live()))

    o_ref[...] = jnp.concatenate([acc_ref[h] for h in range(heads)], axis=1)


def _mix_b_sample(q, cache_kT, cache_vT, nkT, nvT, layer, tkc):
    b, heads, t, d = q.shape
    assert t & (t - 1) == 0 and heads * t == LANES and tkc >= LANES
    assert cache_kT.shape[4] % tkc == 0
    per_b = lambda *blk: pl.BlockSpec((None,) + blk, lambda bb: (bb,) + (0,) * len(blk))
    tri = _strict_lower(tkc)
    buf = (2, heads, d, tkc)
    return pl.pallas_call(
        functools.partial(_mix_b_sample_kernel, layer=layer, tkc=tkc),
        grid=(b,),
        in_specs=[per_b(heads, t, d), per_b(heads, d, t), per_b(heads, d, t), pl.BlockSpec((tkc, tkc), lambda bb: (0, 0)),
                  pl.BlockSpec(memory_space=pl.ANY), pl.BlockSpec(memory_space=pl.ANY)],
        out_specs=per_b(t, heads * d),
        out_shape=jax.ShapeDtypeStruct((b, t, heads * d), F32),
        scratch_shapes=[pltpu.VMEM(buf, F32), pltpu.VMEM(buf, F32), pltpu.SemaphoreType.DMA((2, 2)),
                        pltpu.VMEM((heads * t, 1), F32), pltpu.VMEM((heads, t, d), F32),
                        pltpu.VMEM((2, heads, d, LANES), F32)],
        compiler_params=_params("arbitrary"),
        name="mix_b_sample",
    )(q, nkT, nvT, tri, cache_kT, cache_vT)


def _merge_kernel(x_ref, oa_ref, ob_ref, gate_ref, g1_ref, sh2_ref, sc2_ref, n2_ref,
                  wa_ref, wb_ref, wo_ref, wr_ref, br_ref, x1_o, h2_o, rg_o):
    nb, tr, d = x_ref.shape
    tm = nb * tr
    a = _dot(oa_ref[...], wa_ref[...])
    bb = _dot(ob_ref[...], wb_ref[...])
    merged = gate_ref[:, 0:d] * a + gate_ref[:, d:2 * d] * bb
    mix = _dot(merged.astype(BF16), wo_ref[...])
    x1 = x_ref[...] + g1_ref[...] * mix.reshape(nb, tr, d)
    x1_o[...] = x1
    ms = jnp.mean(x1 * x1, axis=-1, keepdims=True)
    h2 = ((x1 * lax.rsqrt(ms + RMS_EPS) * n2_ref[...]) * (1.0 + sc2_ref[...]) + sh2_ref[...]).reshape(tm, d)
    h2_o[...] = h2.astype(BF16)

    logits = _dot3(h2, wr_ref[...])
    e = jnp.exp(logits - jnp.max(logits, axis=1, keepdims=True))
    aff = e / jnp.sum(e, axis=1, keepdims=True)
    biased = aff + br_ref[...]
    col = [biased[:, j:j + 1] for j in range(N_EXPERTS)]
    best_val = None
    for g in range(N_GROUPS):
        c = col[g * EXP_PER_GROUP:(g + 1) * EXP_PER_GROUP]
        score = None
        for u in range(EXP_PER_GROUP):
            for w in range(u + 1, EXP_PER_GROUP):
                pair = c[u] + c[w]
                score = pair if score is None else jnp.maximum(score, pair)
        if best_val is None:
            best_val, best_g = score, jnp.zeros((tm, 1), I32)
        else:
            upd = score > best_val
            best_g = jnp.where(upd, g, best_g)
            best_val = jnp.where(upd, score, best_val)
    eidx = lax.broadcasted_iota(I32, (tm, N_EXPERTS), 1)
    cand = jnp.where((eidx >> 2) == best_g, biased, -jnp.inf)
    m1 = jnp.max(cand, axis=1, keepdims=True)
    i1 = jnp.min(jnp.where(cand == m1, eidx, N_EXPERTS), axis=1, keepdims=True)
    cand2 = jnp.where(eidx == i1, -jnp.inf, cand)
    m2 = jnp.max(cand2, axis=1, keepdims=True)
    i2 = jnp.min(jnp.where(cand2 == m2, eidx, N_EXPERTS), axis=1, keepdims=True)
    w1 = jnp.sum(jnp.where(eidx == i1, aff, 0.0), axis=1, keepdims=True)
    w2 = jnp.sum(jnp.where(eidx == i2, aff, 0.0), axis=1, keepdims=True)
    tot = w1 + w2
    rg_o[...] = jnp.where(eidx == i1, w1 / tot, 0.0) + jnp.where(eidx == i2, w2 / tot, 0.0)


def _merge(x, oa, ob, gates, mod, layer, boff, n2, wa, wb, wo, wr, br, nb, tr):
    b, t, d = x.shape
    tpb = t // tr
    tm = nb * tr
    rows = b * t
    grid = (b // nb) * tpb

    def modspec(which):
        return pl.BlockSpec((None, None, nb, 1, d), lambda i: (layer, which, boff // nb + i // tpb, 0, 0))

    xspec = pl.BlockSpec((nb, tr, d), lambda i: (i // tpb, i % tpb, 0))
    row = lambda w: pl.BlockSpec((tm, w), lambda i: (i, 0))
    lw = lambda r, c: pl.BlockSpec((None, r, c), lambda i: (layer, 0, 0))
    return pl.pallas_call(
        _merge_kernel, grid=(grid,),
        in_specs=[xspec, row(512), row(512), row(2 * d), modspec(2), modspec(3), modspec(4), lw(1, d),
                  lw(512, d), lw(512, d), lw(d, d),
                  pl.BlockSpec((d, N_EXPERTS), lambda i: (0, 0)), pl.BlockSpec((1, N_EXPERTS), lambda i: (0, 0))],
        out_specs=[xspec, row(d), row(N_EXPERTS)],
        out_shape=[jax.ShapeDtypeStruct((b, t, d), F32), jax.ShapeDtypeStruct((rows, d), BF16),
                   jax.ShapeDtypeStruct((rows, N_EXPERTS), F32)],
        compiler_params=_params("arbitrary"), name="merge",
    )(x, oa, ob, gates, mod, mod, mod, n2, wa, wb, wo, wr, br)


def _moe_kernel(x_ref, h_ref, rg_ref, g2_ref, wg_ref, wu_ref, wd_ref, o_ref, acc_ref):
    nb, tr, d = x_ref.shape
    tm = nb * tr
    e = pl.program_id(1)

    @pl.when(e == 0)
    def _():
        acc_ref[...] = jnp.zeros(acc_ref.shape, F32)

    h = h_ref[...]
    gate = _dot(h, wg_ref[...])
    up = _dot(h, wu_ref[...])
    hidden = (gate * jax.nn.sigmoid(gate)) * up
    y = _dot(hidden.astype(BF16), wd_ref[...])
    eidx = lax.broadcasted_iota(I32, (tm, N_EXPERTS), 1)
    w = jnp.sum(jnp.where(eidx == e, rg_ref[...], 0.0), axis=1, keepdims=True)
    acc_ref[...] += w * y

    @pl.when(e == N_EXPERTS - 1)
    def _():
        o_ref[...] = x_ref[...] + g2_ref[...] * acc_ref[...].reshape(nb, tr, d)


def _moe(x1, h2, rg, mod, layer, boff, wg, wu, wd, nb, tr):
    b, t, d = x1.shape
    tpb = t // tr
    tm = nb * tr
    grid = (b // nb) * tpb
    de = wg.shape[-1]
    xspec = pl.BlockSpec((nb, tr, d), lambda i, e: (i // tpb, i % tpb, 0))
    return pl.pallas_call(
        _moe_kernel, grid=(grid, N_EXPERTS),
        in_specs=[xspec, pl.BlockSpec((tm, d), lambda i, e: (i, 0)), pl.BlockSpec((tm, N_EXPERTS), lambda i, e: (i, 0)),
                  pl.BlockSpec((None, None, nb, 1, d), lambda i, e: (layer, 5, boff // nb + i // tpb, 0, 0)),
                  pl.BlockSpec((None, None, d, de), lambda i, e: (layer, e, 0, 0)),
                  pl.BlockSpec((None, None, d, de), lambda i, e: (layer, e, 0, 0)),
                  pl.BlockSpec((None, None, de, d), lambda i, e: (layer, e, 0, 0))],
        out_specs=xspec,
        out_shape=jax.ShapeDtypeStruct((b, t, d), F32),
        scratch_shapes=[pltpu.VMEM((tm, d), F32)],
        compiler_params=_params("arbitrary", "arbitrary"), name="moe",
    )(x1, h2, rg, mod, wg, wu, wd)


def _rope_tables(pos):
    inv_freq = ROPE_THETA ** (-jnp.arange(0, HEAD_DIM, 2, dtype=F32) / HEAD_DIM)
    ang = pos.astype(F32)[:, None] * inv_freq[None, :]
    c, s = jnp.cos(ang), jnp.sin(ang)
    cos = jnp.concatenate([c, c, c, c], axis=1)
    sin = jnp.concatenate([-s, s, -s, s], axis=1)
    return cos, sin


def _widen_w_in(w_in):
    sizes = (512, 128, 128, 256, 64, 4, 512, 512, 512, 2048)
    pts = [int(v) for v in np.cumsum(sizes)[:-1]]
    qa, ka, va, qi, ki, wi, qb, kb, vb, gl = jnp.split(w_in, pts, axis=-1)
    dup = lambda w: jnp.concatenate([w[..., 0:64], w[..., 0:64], w[..., 64:128], w[..., 64:128]], axis=-1)
    wi_pad = jnp.pad(wi, ((0, 0), (0, 0), (0, LANES - wi.shape[-1])))
    ext = jnp.concatenate([qa, dup(ka), dup(va), qi, ki, ki, wi_pad, qb, kb, vb, gl], axis=-1)
    assert ext.shape[-1] == N_EXT
    return ext.astype(BF16)


def kernel(x_prompt, x_sample, cache_a_k, cache_a_v, cache_idx_k, cache_b_k, cache_b_v, c_prompt, c_sample,
           norm1_g, norm2_g, w_ada, b_ada, w_in, qn_g, kn_g, w_branch_a, w_branch_b, w_out, w_router, b_router,
           w_exp_gate, w_exp_up, w_exp_down):
    depth = w_in.shape[0]
    bp, tp, d = x_prompt.shape
    bs, ts, _ = x_sample.shape
    past = cache_a_k.shape[2]

    nc = -(-(bs + bp) // 8) * 8
    c_all = jnp.concatenate([c_sample, c_prompt, jnp.zeros((nc - bs - bp, d), F32)], axis=0)
    mod = _ada(c_all, w_ada, b_ada).reshape(depth, 6, nc, 1, d)
    boff_s, boff_p = 0, bs

    w_ext = _widen_w_in(w_in)
    wa, wb, wo = w_branch_a.astype(BF16), w_branch_b.astype(BF16), w_out.astype(BF16)
    wg, wu, wd = w_exp_gate.astype(BF16), w_exp_up.astype(BF16), w_exp_down.astype(BF16)
    n1 = norm1_g.reshape(depth, 1, d)
    n2 = norm2_g.reshape(depth, 1, d)
    qn = jnp.tile(qn_g, (1, 2)).reshape(depth, 1, LANES)
    kn = jnp.tile(kn_g, (1, 2)).reshape(depth, 1, LANES)
    seg = jnp.asarray(np.kron(np.eye(2), np.ones((64, 64))).astype(np.float32), dtype=BF16)
    br = b_router.reshape(1, N_EXPERTS)

    cos_p, sin_p = _rope_tables(jnp.arange(tp, dtype=I32))
    cos_s, sin_s = _rope_tables(past + jnp.arange(ts, dtype=I32))
    cos_s, sin_s = jnp.tile(cos_s, (bs, 1)), jnp.tile(sin_s, (bs, 1))

    tr_p = min(256, tp)
    tq_a = min(256, tp)
    tk_a = min(512, tp)
    tq_b = min(256, tp)
    tm_moe = min(1024, tp)
    tk_s = min(512, past)

    cache_b_kT = jnp.transpose(cache_b_k, (0, 1, 3, 4, 2))
    cache_b_vT = jnp.transpose(cache_b_v, (0, 1, 3, 4, 2))
    cache_a_kT = jnp.transpose(cache_a_k, (0, 1, 3, 4, 2))
    cache_a_vT = jnp.transpose(cache_a_v, (0, 1, 3, 4, 2))
    cache_idx_kT = jnp.transpose(cache_idx_k, (0, 1, 3, 2))

    xp, xs = x_prompt, x_sample
    rows_p, rows_s = [], []
    for l in range(depth):
        p = _in_proj(xp, mod, l, boff_p, n1, w_ext, cos_p, sin_p, qn, kn, seg, 1, tr_p, True)
        oa = _mix_a_prompt(p, bp, tp, tq_a, tk_a)
        ob = _mix_b_prompt(p, bp, tp, tq_b)
        x1, h2, rg = _merge(xp, oa, ob, p["gates"], mod, l, boff_p, n2, wa, wb, wo, w_router, br, 1, tr_p)
        xp = _moe(x1, h2, rg, mod, l, boff_p, wg, wu, wd, 1, tm_moe)
        rows_p.append(p)

        s = _in_proj(xs, mod, l, boff_s, n1, w_ext, cos_s, sin_s, qn, kn, seg, bs, ts, False)
        qi_hm = s["qi"].reshape(bs, ts, IDX_HEADS, IDX_DIM).transpose(0, 2, 1, 3).reshape(bs, IDX_HEADS * ts, IDX_DIM)
        qa5 = s["qa"].reshape(bs, ts, A_KV_HEADS, A_REP, HEAD_DIM).transpose(0, 2, 3, 1, 4)
        qa5 = qa5.reshape(bs, A_KV_HEADS, A_REP * ts, HEAD_DIM)
        kv_major = lambda a: a.reshape(bs, ts, A_KV_HEADS, HEAD_DIM).transpose(0, 2, 3, 1)
        oa_s = _mix_a_sample(qi_hm, s["wi"].reshape(bs, ts, WI_PAD), cache_idx_kT,
                             s["nik"].reshape(bs, ts, IDX_DIM).transpose(0, 2, 1), qa5, cache_a_kT, cache_a_vT,
                             kv_major(s["nak"]), kv_major(s["nav"]), l, tk_s)
        oa_s = oa_s.reshape(bs, A_KV_HEADS, A_REP, ts, HEAD_DIM).transpose(0, 3, 1, 2, 4).reshape(bs * ts, 512)
        heads_first = lambda a: a.reshape(bs, ts, B_HEADS, HEAD_DIM).transpose(0, 2, 1, 3)
        feature_major = lambda a: a.reshape(bs, ts, B_HEADS, HEAD_DIM).transpose(0, 2, 3, 1)
        ob_s = _mix_b_sample(heads_first(s["qb"]), cache_b_kT, cache_b_vT, feature_major(s["nbk"]),
                             feature_major(s["nbv"]), l, min(256, past))
        x1s, h2s, rgs = _merge(xs, oa_s.astype(BF16), ob_s.reshape(bs * ts, 512).astype(BF16), s["gates"], mod, l,
                               boff_s, n2, wa, wb, wo, w_router, br, bs, ts)
        xs = _moe(x1s, h2s, rgs, mod, l, boff_s, wg, wu, wd, bs, ts)
        rows_s.append(s)

    def stack(rows, name, b, t, tail):
        return jnp.stack([r[name].reshape((b, t) + tail) for r in rows], axis=0)

    def stack_t(rows, name, tail):
        a = jnp.stack([r[name] for r in rows], axis=0).reshape((depth, bp) + tail + (tp,))
        return jnp.moveaxis(a, -1, 2)

    kv = (A_KV_HEADS, HEAD_DIM)
    bh = (B_HEADS, HEAD_DIM)
    return (xp, xs,
            stack_t(rows_p, "nak", kv), stack_t(rows_p, "nav", kv), stack_t(rows_p, "nik", (IDX_DIM,)),
            stack_t(rows_p, "nbk", bh), stack_t(rows_p, "nbv", bh),
            stack(rows_s, "nak", bs, ts, kv), stack(rows_s, "nav", bs, ts, kv), stack(rows_s, "nik", bs, ts, (IDX_DIM,)),
            stack(rows_s, "nbk", bs, ts, bh), stack(rows_s, "nbv", bs, ts, bh))
```
